```python
import math
import jax, jax.numpy as jnp
from jax import lax
import numpy as np

D_MODEL = 1024
BATCH = 32
SEQ = 2048
DEPTH = 4

CHUNK = 64
RET_HEADS = 4
RET_DK = 128
RET_DV = 256
CA_HEADS = 8
CA_DH = 64
CA_LEFT_CHUNKS = 8
REL_CLIP = 256
DA_HEADS = 4
DA_DH = 64
Q_BLOCK = 128
FFN_HIDDEN = -(-8 * D_MODEL // (3 * 256)) * 256
ROPE_THETA = 10000.0
LN_EPS = 1e-5
NEG_INF = -1e30
DEEPNORM_ALPHA = (2.0 * DEPTH) ** 0.25
DEEPNORM_BETA = (8.0 * DEPTH) ** -0.25
N_BRANCHES = 3

RET_QK_W = RET_HEADS * RET_DK
RET_V_W = RET_HEADS * RET_DV
CA_W = CA_HEADS * CA_DH
DA_QK_W = 2 * DA_HEADS * DA_DH
DA_V_W = DA_HEADS * 2 * DA_DH
IN_WIDTHS = (RET_QK_W, RET_QK_W, RET_V_W, RET_V_W, CA_W, CA_W, CA_W, DA_QK_W, DA_QK_W, DA_V_W, N_BRANCHES * D_MODEL)
IN_IS_VALUE = (False, False, True, False, False, False, True, False, False, True, False)
IN_TOTAL = sum(IN_WIDTHS)

kernel_name = "hybrid_retention_chunkattn_diffattn_deepnorm"


def _layer_norm(x, g, b):
    xf = x.astype(jnp.float32)
    mu = jnp.mean(xf, axis=-1, keepdims=True)
    var = jnp.mean(jnp.square(xf - mu), axis=-1, keepdims=True)
    return ((xf - mu) * lax.rsqrt(var + LN_EPS) * g.astype(jnp.float32) + b.astype(jnp.float32)).astype(x.dtype)


def _rms_norm(x, g):
    xf = x.astype(jnp.float32)
    y = xf * lax.rsqrt(jnp.mean(xf * xf, axis=-1, keepdims=True) + LN_EPS) * g.astype(jnp.float32)
    return y.astype(x.dtype)


def _rotary(x):
    S, d = x.shape[1], x.shape[-1]
    inv_freq = ROPE_THETA ** (-jnp.arange(0, d, 2, dtype=jnp.float32) / d)
    ang = jnp.arange(S, dtype=jnp.float32)[:, None] * inv_freq[None, :]
    cos = jnp.cos(ang)[None, :, None, :].astype(x.dtype)
    sin = jnp.sin(ang)[None, :, None, :].astype(x.dtype)
    x1, x2 = x[..., : d // 2], x[..., d // 2:]
    return jnp.concatenate([x1 * cos - x2 * sin, x2 * cos + x1 * sin], axis=-1)


def _retention(q, k, v, gate, norm_g):
    B, S, H, dk = q.shape
    dv = v.shape[-1]
    nc = S // CHUNK
    q = _rotary(q)
    k = _rotary(k) * (dk ** -0.5)
    log_gamma = jnp.log1p(-jnp.exp2(-5.0 - jnp.arange(H, dtype=jnp.float32)))
    pos = jnp.arange(CHUNK, dtype=jnp.float32)
    intra_decay = jnp.exp(log_gamma[:, None, None] * jnp.abs(pos[:, None] - pos[None, :]))
    q_decay = jnp.exp(log_gamma[None, :] * (pos[:, None] + 1.0))
    k_decay = jnp.exp(log_gamma[None, :] * (CHUNK - 1.0 - pos[:, None]))
    chunk_decay = jnp.exp(log_gamma * CHUNK)

    qc = q.reshape(B, nc, CHUNK, H, dk)
    kc = k.reshape(B, nc, CHUNK, H, dk)
    vc = v.reshape(B, nc, CHUNK, H, dv)
    scores = jnp.einsum('bcnhd,bcmhd->bchnm', qc, kc) * intra_decay.astype(q.dtype)
    o_intra = jnp.einsum('bchnm,bcmhe->bcnhe', scores, vc)

    def step(state, xs):
        q_i, k_i, v_i = xs
        o = jnp.einsum('bnhd,bhde->bnhe', q_i.astype(jnp.float32) * q_decay[None, :, :, None], state)
        kv = jnp.einsum('bmhd,bmhe->bhde', k_i.astype(jnp.float32) * k_decay[None, :, :, None], v_i.astype(jnp.float32))
        state = state * chunk_decay[None, :, None, None] + kv
        return state, o

    state0 = jnp.zeros((B, H, dk, dv), jnp.float32)
    _, o_inter = lax.scan(step, state0, (jnp.swapaxes(qc, 0, 1), jnp.swapaxes(kc, 0, 1), jnp.swapaxes(vc, 0, 1)))
    o = (o_intra.astype(jnp.float32) + jnp.swapaxes(o_inter, 0, 1)).astype(v.dtype)
    o = _rms_norm(o, norm_g).reshape(B, S, H * dv)
    return jax.nn.silu(gate) * o


def _chunk_attention(q, k, v, rel_bias):
    B, S, H, d = q.shape
    nc = S // CHUNK
    pad = CA_LEFT_CHUNKS * CHUNK
    band = pad + CHUNK
    k_pad = jnp.pad(k, ((0, 0), (pad, 0), (0, 0), (0, 0)))
    v_pad = jnp.pad(v, ((0, 0), (pad, 0), (0, 0), (0, 0)))
    qpos = jnp.arange(CHUNK)
    kpos = jnp.arange(band)
    rel = qpos[:, None] + pad - kpos[None, :]
    idx = jnp.clip(rel, -REL_CLIP, REL_CLIP) + REL_CLIP
    bias = rel_bias[:, idx].astype(jnp.float32)
    q = q * (d ** -0.5)

    def one_chunk(i):
        start = i * CHUNK
        q_i = lax.dynamic_slice_in_dim(q, start, CHUNK, axis=1)
        k_i = lax.dynamic_slice_in_dim(k_pad, start, band, axis=1)
        v_i = lax.dynamic_slice_in_dim(v_pad, start, band, axis=1)
        s = jnp.einsum('bqhd,bkhd->bhqk', q_i, k_i).astype(jnp.float32) + bias[None]
        valid = (start - pad + kpos) >= 0
        s = jnp.where(valid[None, None, None, :], s, NEG_INF)
        p = jax.nn.softmax(s, axis=-1).astype(v.dtype)
        return jnp.einsum('bhqk,bkhd->bqhd', p, v_i)

    o = lax.map(one_chunk, jnp.arange(nc))
    return jnp.transpose(o, (1, 0, 2, 3, 4)).reshape(B, S, H * d)


def _diff_attention(q, k, v, lam, lambda_init, norm_g):
    B, S, H2, d = q.shape
    H = H2 // 2
    q = _rotary(q) * (d ** -0.5)
    k = _rotary(k)
    nb = S // Q_BLOCK
    key_chunk = jnp.arange(S) // CHUNK

    def one_block(i):
        start = i * Q_BLOCK
        q_i = lax.dynamic_slice_in_dim(q, start, Q_BLOCK, axis=1)
        s = jnp.einsum('bqgd,bkgd->bgqk', q_i, k).astype(jnp.float32)
        query_chunk = (start + jnp.arange(Q_BLOCK)) // CHUNK
        allowed = key_chunk[None, :] <= query_chunk[:, None]
        s = jnp.where(allowed[None, None], s, NEG_INF)
        p = jax.nn.softmax(s, axis=-1).reshape(B, H, 2, Q_BLOCK, S)
        p = p[:, :, 0] - lam * p[:, :, 1]
        return jnp.einsum('bhqk,bkhe->bqhe', p.astype(v.dtype), v)

    o = lax.map(one_block, jnp.arange(nb))
    o = jnp.transpose(o, (1, 0, 2, 3, 4)).reshape(B, S, H, 2 * d)
    o = _rms_norm(o, norm_g) * (1.0 - lambda_init)
    return o.reshape(B, S, H * 2 * d)


def setup_inputs(seed: int = 0) -> dict:
    key = jax.random.key(seed)
    ks = jax.random.split(key, 24)

    def nrm(k, shape, scale):
        return jax.random.normal(k, shape, jnp.float32) * scale

    beta = DEEPNORM_BETA
    col_scale = jnp.concatenate([
        jnp.full((w,), beta if is_v else 1.0, jnp.float32) for w, is_v in zip(IN_WIDTHS, IN_IS_VALUE)])
    x = nrm(ks[0], (BATCH, SEQ, D_MODEL), 1.0)
    w_in = nrm(ks[1], (DEPTH, D_MODEL, IN_TOTAL), D_MODEL ** -0.5) * col_scale
    ret_norm_g = 1.0 + nrm(ks[2], (DEPTH, RET_DV), 0.02)
    ca_rel_bias = nrm(ks[3], (DEPTH, CA_HEADS, 2 * REL_CLIP + 1), 0.1)
    da_lambda_q1 = nrm(ks[4], (DEPTH, DA_DH), 0.1)
    da_lambda_k1 = nrm(ks[5], (DEPTH, DA_DH), 0.1)
    da_lambda_q2 = nrm(ks[6], (DEPTH, DA_DH), 0.1)
    da_lambda_k2 = nrm(ks[7], (DEPTH, DA_DH), 0.1)
    da_norm_g = 1.0 + nrm(ks[8], (DEPTH, 2 * DA_DH), 0.02)
    w_branch_a = nrm(ks[9], (DEPTH, RET_V_W, D_MODEL), RET_V_W ** -0.5 * beta)
    w_branch_b = nrm(ks[10], (DEPTH, CA_W, D_MODEL), CA_W ** -0.5 * beta)
    w_branch_c = nrm(ks[11], (DEPTH, DA_V_W, D_MODEL), DA_V_W ** -0.5 * beta)
    b_merge = nrm(ks[12], (DEPTH, N_BRANCHES * D_MODEL), 0.1)
    w_out = nrm(ks[13], (DEPTH, D_MODEL, D_MODEL), D_MODEL ** -0.5 * beta)
    ln1_g = 1.0 + nrm(ks[14], (DEPTH, D_MODEL), 0.02)
    ln1_b = nrm(ks[15], (DEPTH, D_MODEL), 0.02)
    w_ffn_in = nrm(ks[16], (DEPTH, D_MODEL, 2 * FFN_HIDDEN), D_MODEL ** -0.5 * beta)
    w_ffn_out = nrm(ks[17], (DEPTH, FFN_HIDDEN, D_MODEL), FFN_HIDDEN ** -0.5 * beta)
    ln2_g = 1.0 + nrm(ks[18], (DEPTH, D_MODEL), 0.02)
    ln2_b = nrm(ks[19], (DEPTH, D_MODEL), 0.02)
    return {
        "x": x, "w_in": w_in, "ret_norm_g": ret_norm_g, "ca_rel_bias": ca_rel_bias,
        "da_lambda_q1": da_lambda_q1, "da_lambda_k1": da_lambda_k1,
        "da_lambda_q2": da_lambda_q2, "da_lambda_k2": da_lambda_k2, "da_norm_g": da_norm_g,
        "w_branch_a": w_branch_a, "w_branch_b": w_branch_b, "w_branch_c": w_branch_c,
        "b_merge": b_merge, "w_out": w_out, "ln1_g": ln1_g, "ln1_b": ln1_b,
        "w_ffn_in": w_ffn_in, "w_ffn_out": w_ffn_out, "ln2_g": ln2_g, "ln2_b": ln2_b,
    }


def reference(x, w_in, ret_norm_g, ca_rel_bias, da_lambda_q1, da_lambda_k1, da_lambda_q2, da_lambda_k2,
              da_norm_g, w_branch_a, w_branch_b, w_branch_c, b_merge, w_out, ln1_g, ln1_b,
              w_ffn_in, w_ffn_out, ln2_g, ln2_b):
    B, S, D = x.shape
    splits = [int(s) for s in np.cumsum(IN_WIDTHS)[:-1]]
    for l in range(DEPTH):
        h = jnp.einsum('bsd,de->bse', x, w_in[l])
        rq, rk, rv, rg, cq, ck, cv, dq, dk, dv, gates = jnp.split(h, splits, axis=-1)

        o_a = _retention(rq.reshape(B, S, RET_HEADS, RET_DK), rk.reshape(B, S, RET_HEADS, RET_DK),
                         rv.reshape(B, S, RET_HEADS, RET_DV), rg, ret_norm_g[l])
        o_b = _chunk_attention(cq.reshape(B, S, CA_HEADS, CA_DH), ck.reshape(B, S, CA_HEADS, CA_DH),
                               cv.reshape(B, S, CA_HEADS, CA_DH), ca_rel_bias[l])
        lambda_init = 0.8 - 0.6 * math.exp(-0.3 * l)
        lam = (jnp.exp(jnp.sum(da_lambda_q1[l].astype(jnp.float32) * da_lambda_k1[l].astype(jnp.float32)))
               - jnp.exp(jnp.sum(da_lambda_q2[l].astype(jnp.float32) * da_lambda_k2[l].astype(jnp.float32)))
               + lambda_init)
        o_c = _diff_attention(dq.reshape(B, S, 2 * DA_HEADS, DA_DH), dk.reshape(B, S, 2 * DA_HEADS, DA_DH),
                              dv.reshape(B, S, DA_HEADS, 2 * DA_DH), lam, lambda_init, da_norm_g[l])

        g = jax.nn.sigmoid(gates + b_merge[l]).reshape(B, S, N_BRANCHES, D)
        merged = (g[:, :, 0] * jnp.einsum('bse,ed->bsd', o_a, w_branch_a[l])
                  + g[:, :, 1] * jnp.einsum('bse,ed->bsd', o_b, w_branch_b[l])
                  + g[:, :, 2] * jnp.einsum('bse,ed->bsd', o_c, w_branch_c[l]))
        mix = jnp.einsum('bsd,de->bse', merged, w_out[l])
        x = _layer_norm(DEEPNORM_ALPHA * x + mix, ln1_g[l], ln1_b[l])

        u = jnp.einsum('bsd,df->bsf', x, w_ffn_in[l])
        u_gate, u_up = jnp.split(u, 2, axis=-1)
        ffn = jnp.einsum('bsf,fd->bsd', jax.nn.silu(u_gate) * u_up, w_ffn_out[l])
        x = _layer_norm(DEEPNORM_ALPHA * x + ffn, ln2_g[l], ln2_b[l])
    return x
```

```python
import functools
import math

import jax
import jax.numpy as jnp
import numpy as np
from jax import lax
from jax.experimental import pallas as pl
from jax.experimental.pallas import tpu as pltpu

D_MODEL = 1024
DEPTH = 4
CHUNK = 64
RET_HEADS, RET_DK, RET_DV = 4, 128, 256
CA_HEADS, CA_DH, CA_LEFT_CHUNKS, REL_CLIP = 8, 64, 8, 256
DA_HEADS, DA_DH = 4, 64
FFN_HIDDEN = -(-8 * D_MODEL // (3 * 256)) * 256
ROPE_THETA = 10000.0
LN_EPS = 1e-5
NEG_INF = -1e30
DEEPNORM_ALPHA = (2.0 * DEPTH) ** 0.25

RET_QK_W = RET_HEADS * RET_DK
RET_V_W = RET_HEADS * RET_DV
CA_W = CA_HEADS * CA_DH
DA_QK_W = 2 * DA_HEADS * DA_DH
DA_V_W = DA_HEADS * 2 * DA_DH
OFF_RQ = 0
OFF_RK = OFF_RQ + RET_QK_W
OFF_RV = OFF_RK + RET_QK_W
OFF_RG = OFF_RV + RET_V_W
OFF_CQ = OFF_RG + RET_V_W
OFF_CK = OFF_CQ + CA_W
OFF_CV = OFF_CK + CA_W
OFF_DQ = OFF_CV + CA_W
OFF_DK = OFF_DQ + DA_QK_W
OFF_DV = OFF_DK + DA_QK_W
MIX_W = OFF_DV + DA_V_W
GATE_W = 3 * D_MODEL

LANES = 128
VMEM_LIMIT = 56 * 1024 * 1024

RET_BLOCK = 256
CA_QBLOCK = 128
CA_PAD = CA_LEFT_CHUNKS * CHUNK
CA_WIN = CA_PAD + CA_QBLOCK
DA_BLOCK = 256

BF16 = jnp.bfloat16
F32 = jnp.float32


def _dot(a, b):
    return jnp.dot(a, b, preferred_element_type=F32)


def _dot_nt(a, b):
    return lax.dot_general(a, b, (((1,), (1,)), ((), ())), preferred_element_type=F32)


def _dot_tn(a, b):
    return lax.dot_general(a, b, (((0,), (0,)), ((), ())), preferred_element_type=F32)


def _const_spec(shape):
    zeros = (0,) * len(shape)
    return pl.BlockSpec(shape, lambda *_: zeros, pipeline_mode=pl.Buffered(1))


def _params(n_axes):
    return pltpu.CompilerParams(dimension_semantics=("arbitrary",) * n_axes, vmem_limit_bytes=VMEM_LIMIT)


def _inproj_kernel(x_ref, w_ref, rcos_ref, rsin_ref, dcos_ref, dsina_ref, dsinb_ref, o_ref):
    xb = x_ref[...].astype(BF16)

    def proj(off, width):
        return _dot(xb, w_ref[:, off:off + width])

    def ret_rotary(a, scale):
        outs = []
        for h in range(RET_HEADS):
            ah = a[:, h * RET_DK:(h + 1) * RET_DK]
            r = ah * rcos_ref[...] + pltpu.roll(ah, RET_DK // 2, 1) * rsin_ref[...]
            outs.append(r * scale if scale != 1.0 else r)
        return jnp.concatenate(outs, axis=1)

    def da_rotary(a, scale):
        outs = []
        for g in range(DA_QK_W // LANES):
            ag = a[:, g * LANES:(g + 1) * LANES]
            r = (ag * dcos_ref[...] + pltpu.roll(ag, LANES - DA_DH // 2, 1) * dsina_ref[...]
                 + pltpu.roll(ag, DA_DH // 2, 1) * dsinb_ref[...])
            outs.append(r * scale if scale != 1.0 else r)
        return jnp.concatenate(outs, axis=1)

    o_ref[:, OFF_RQ:OFF_RQ + RET_QK_W] = ret_rotary(proj(OFF_RQ, RET_QK_W), 1.0).astype(BF16)
    o_ref[:, OFF_RK:OFF_RK + RET_QK_W] = ret_rotary(proj(OFF_RK, RET_QK_W), RET_DK ** -0.5).astype(BF16)
    for off in range(OFF_RV, OFF_CQ, 512):
        o_ref[:, off:off + 512] = proj(off, 512).astype(BF16)
    o_ref[:, OFF_CQ:OFF_CQ + CA_W] = (proj(OFF_CQ, CA_W) * (CA_DH ** -0.5)).astype(BF16)
    o_ref[:, OFF_CK:OFF_CK + CA_W] = proj(OFF_CK, CA_W).astype(BF16)
    o_ref[:, OFF_CV:OFF_CV + CA_W] = proj(OFF_CV, CA_W).astype(BF16)
    o_ref[:, OFF_DQ:OFF_DQ + DA_QK_W] = da_rotary(proj(OFF_DQ, DA_QK_W), DA_DH ** -0.5).astype(BF16)
    o_ref[:, OFF_DK:OFF_DK + DA_QK_W] = da_rotary(proj(OFF_DK, DA_QK_W), 1.0).astype(BF16)
    o_ref[:, OFF_DV:OFF_DV + DA_V_W] = proj(OFF_DV, DA_V_W).astype(BF16)


def _inproj(x2, w_mix, tabs, seq, tm):
    t = x2.shape[0]
    pos_blocks = seq // tm
    tab_spec = pl.BlockSpec((tm, LANES), lambda i: (i % pos_blocks, 0))
    return pl.pallas_call(
        _inproj_kernel,
        grid=(t // tm,),
        in_specs=[pl.BlockSpec((tm, D_MODEL), lambda i: (i, 0)),
                  _const_spec((D_MODEL, MIX_W)),
                  tab_spec, tab_spec, tab_spec, tab_spec, tab_spec],
        out_specs=pl.BlockSpec((tm, MIX_W), lambda i: (i, 0)),
        out_shape=jax.ShapeDtypeStruct((t, MIX_W), BF16),
        compiler_params=_params(1),
        name="inproj",
    )(x2, w_mix, *tabs)


def _retention_kernel(q_ref, k_ref, v_ref, g_ref, dmat_ref, qdec_ref, kdec_ref, cdec_ref, ng_ref, o_ref, state_ref):
    n_blocks = q_ref.shape[0] // RET_BLOCK
    state_ref[...] = jnp.zeros_like(state_ref)

    def step(c, carry):
        r0 = pl.multiple_of(c * RET_BLOCK, RET_BLOCK)
        rows = pl.ds(r0, RET_BLOCK)
        q = q_ref[rows, :]
        k = k_ref[rows, :]
        v = v_ref[rows, :]
        s = _dot_nt(q, k) * dmat_ref[...]
        o = _dot(s.astype(BF16), v)
        state = state_ref[...]
        qd = (q.astype(F32) * qdec_ref[...]).astype(BF16)
        o = o + _dot(qd, state.astype(BF16))
        kd = (k.astype(F32) * kdec_ref[...]).astype(BF16)
        state_ref[...] = state * cdec_ref[0:1, :] + _dot_tn(kd, v)
        ms = jnp.mean(o * o, axis=-1, keepdims=True)
        y = o * lax.rsqrt(ms + LN_EPS) * ng_ref[...]
        gate = g_ref[rows, :].astype(F32)
        o_ref[rows, :] = (gate / (1.0 + jnp.exp(-gate)) * y).astype(BF16)
        return carry

    lax.fori_loop(0, n_blocks, step, 0)


def _retention(h3, tabs, norm_g):
    b, s, _ = h3.shape
    dmat, qdec, kdec, cdec = tabs
    return pl.pallas_call(
        _retention_kernel,
        grid=(b, RET_HEADS),
        in_specs=[pl.BlockSpec((None, s, RET_DK), lambda i, h: (i, 0, OFF_RQ // RET_DK + h)),
                  pl.BlockSpec((None, s, RET_DK), lambda i, h: (i, 0, OFF_RK // RET_DK + h)),
                  pl.BlockSpec((None, s, RET_DV), lambda i, h: (i, 0, OFF_RV // RET_DV + h)),
                  pl.BlockSpec((None, s, RET_DV), lambda i, h: (i, 0, OFF_RG // RET_DV + h)),
                  pl.BlockSpec((None, RET_BLOCK, RET_BLOCK), lambda i, h: (h, 0, 0)),
                  pl.BlockSpec((None, RET_BLOCK, RET_DK), lambda i, h: (h, 0, 0)),
                  pl.BlockSpec((None, RET_BLOCK, RET_DK), lambda i, h: (h, 0, 0)),
                  pl.BlockSpec((None, 8, RET_DV), lambda i, h: (h, 0, 0)),
                  pl.BlockSpec((1, RET_DV), lambda i, h: (0, 0))],
        out_specs=pl.BlockSpec((None, s, RET_DV), lambda i, h: (i, 0, h)),
        out_shape=jax.ShapeDtypeStruct((b, s, RET_V_W), BF16),
        scratch_shapes=[pltpu.VMEM((RET_DK, RET_DV), F32)],
        compiler_params=_params(2),
        name="retention",
    )(h3, h3, h3, h3, dmat, qdec, kdec, cdec, norm_g)


def _retention_tables():
    h = np.arange(RET_HEADS, dtype=np.float64)
    log_gamma = np.log1p(-np.exp2(-5.0 - h))
    n = np.arange(RET_BLOCK)
    diff = (n[:, None] - n[None, :]).astype(np.float64)
    cn, cm = n[:, None] // CHUNK, n[None, :] // CHUNK
    expo = np.where(cm == cn, np.abs(diff), diff)
    dmat = np.where(cm <= cn, np.exp(log_gamma[:, None, None] * expo[None]), 0.0)
    qdec = np.exp(log_gamma[:, None] * (n[None, :] + 1.0))
    kdec = np.exp(log_gamma[:, None] * (RET_BLOCK - 1.0 - n[None, :]))
    cdec = np.exp(log_gamma * RET_BLOCK)
    qdec = np.broadcast_to(qdec[:, :, None], (RET_HEADS, RET_BLOCK, RET_DK))
    kdec = np.broadcast_to(kdec[:, :, None], (RET_HEADS, RET_BLOCK, RET_DK))
    cdec = np.broadcast_to(cdec[:, None, None], (RET_HEADS, 8, RET_DV))
    return tuple(jnp.asarray(a, F32) for a in (dmat, qdec, kdec, cdec))


def _chunkattn_kernel(q_ref, k_ref, v_ref, bias_ref, o_ref, kpad_ref, vpad_ref):
    s_len = q_ref.shape[0]
    n_blocks = s_len // CA_QBLOCK
    kpad_ref[0:CA_PAD, :] = jnp.zeros((CA_PAD, CA_W), BF16)
    vpad_ref[0:CA_PAD, :] = jnp.zeros((CA_PAD, CA_W), BF16)
    kpad_ref[CA_PAD:, :] = k_ref[...]
    vpad_ref[CA_PAD:, :] = v_ref[...]
    col = lax.broadcasted_iota(jnp.int32, (CA_QBLOCK, CA_WIN), 1)

    def qblock(i, carry):
        r0 = pl.multiple_of(i * CA_QBLOCK, CA_QBLOCK)
        q = q_ref[pl.ds(r0, CA_QBLOCK), :]
        kw = kpad_ref[pl.ds(r0, CA_WIN), :]
        vw = vpad_ref[pl.ds(r0, CA_WIN), :]
        in_seq = (col + r0) >= CA_PAD
        outs = []
        for h in range(CA_HEADS):
            lo, hi = h * CA_DH, (h + 1) * CA_DH
            s = _dot_nt(q[:, lo:hi], kw[:, lo:hi]) + bias_ref[h]
            s = jnp.where(in_seq, s, NEG_INF)
            m = jnp.max(s, axis=-1, keepdims=True)
            p = jnp.exp(s - m)
            l = jnp.sum(p, axis=-1, keepdims=True)
            outs.append(_dot(p.astype(BF16), vw[:, lo:hi]) * (1.0 / l))
        o_ref[pl.ds(r0, CA_QBLOCK), :] = jnp.concatenate(outs, axis=1).astype(BF16)
        return carry

    lax.fori_loop(0, n_blocks, qblock, 0)


def _chunkattn(h3, bias):
    b, s, _ = h3.shape
    return pl.pallas_call(
        _chunkattn_kernel,
        grid=(b,),
        in_specs=[pl.BlockSpec((None, s, CA_W), lambda i: (i, 0, OFF_CQ // CA_W)),
                  pl.BlockSpec((None, s, CA_W), lambda i: (i, 0, OFF_CK // CA_W)),
                  pl.BlockSpec((None, s, CA_W), lambda i: (i, 0, OFF_CV // CA_W)),
                  _const_spec((CA_HEADS, CA_QBLOCK, CA_WIN))],
        out_specs=pl.BlockSpec((None, s, CA_W), lambda i: (i, 0, 0)),
        out_shape=jax.ShapeDtypeStruct((b, s, CA_W), BF16),
        scratch_shapes=[pltpu.VMEM((s + CA_PAD, CA_W), BF16), pltpu.VMEM((s + CA_PAD, CA_W), BF16)],
        compiler_params=_params(1),
        name="chunkattn",
    )(h3, h3, h3, bias)


def _chunkattn_bias(rel_bias):
    r = np.arange(CA_QBLOCK)[:, None]
    j = np.arange(CA_WIN)[None, :]
    rel = r + CA_PAD - j
    idx = np.clip(rel, -REL_CLIP, REL_CLIP) + REL_CLIP
    qc, kc = r // CHUNK, j // CHUNK
    band = (kc >= qc) & (kc <= qc + CA_LEFT_CHUNKS)
    return jnp.where(jnp.asarray(band)[None], rel_bias[:, idx].astype(F32), NEG_INF)


def _diffattn_kernel(lambda_init, q_ref, k_ref, v_ref, lq1_ref, lk1_ref, lq2_ref, lk2_ref, ng_ref, o_ref):
    s_len = q_ref.shape[0]
    n_blocks = s_len // DA_BLOCK
    lam = (jnp.exp(jnp.sum(lq1_ref[...] * lk1_ref[...], axis=-1, keepdims=True))
           - jnp.exp(jnp.sum(lq2_ref[...] * lk2_ref[...], axis=-1, keepdims=True)) + lambda_init)
    row = lax.broadcasted_iota(jnp.int32, (DA_BLOCK, DA_BLOCK), 0) // CHUNK
    colc = lax.broadcasted_iota(jnp.int32, (DA_BLOCK, DA_BLOCK), 1) // CHUNK
    diag_allowed = colc <= row

    def qblock(i, carry):
        r0 = pl.multiple_of(i * DA_BLOCK, DA_BLOCK)
        q = q_ref[pl.ds(r0, DA_BLOCK), :]
        qs = (q[:, :DA_DH], q[:, DA_DH:])

        def update(j, st, masked):
            c0 = pl.multiple_of(j * DA_BLOCK, DA_BLOCK)
            k = k_ref[pl.ds(c0, DA_BLOCK), :]
            v = v_ref[pl.ds(c0, DA_BLOCK), :]
            ks = (k[:, :DA_DH], k[:, DA_DH:])
            new = []
            for t in range(2):
                m_old, l_old, acc_old = st[t]
                s = _dot_nt(qs[t], ks[t])
                if masked:
                    s = jnp.where(diag_allowed, s, NEG_INF)
                m_new = jnp.maximum(m_old, jnp.max(s, axis=-1, keepdims=True))
                a = jnp.exp(m_old - m_new)
                p = jnp.exp(s - m_new)
                l_new = l_old * a + jnp.sum(p, axis=-1, keepdims=True)
                acc_new = acc_old * a + _dot(p.astype(BF16), v)
                new.append((m_new, l_new, acc_new))
            return tuple(new)

        init = tuple((jnp.full((DA_BLOCK, 1), NEG_INF, F32), jnp.zeros((DA_BLOCK, 1), F32),
                      jnp.zeros((DA_BLOCK, 2 * DA_DH), F32)) for _ in range(2))
        st = lax.fori_loop(0, i, lambda j, st: update(j, st, False), init)
        st = update(i, st, True)
        (_, l1, acc1), (_, l2, acc2) = st
        o = acc1 * (1.0 / l1) - lam * (acc2 * (1.0 / l2))
        ms = jnp.mean(o * o, axis=-1, keepdims=True)
        y = o * lax.rsqrt(ms + LN_EPS) * ng_ref[...] * (1.0 - lambda_init)
        o_ref[pl.ds(r0, DA_BLOCK), :] = y.astype(BF16)
        return carry

    lax.fori_loop(0, n_blocks, qblock, 0)


def _diffattn(h3, lams, norm_g, lambda_init):
    b, s, _ = h3.shape
    hw = 2 * DA_DH
    small = pl.BlockSpec((1, DA_DH), lambda i, h: (0, 0))
    return pl.pallas_call(
        functools.partial(_diffattn_kernel, lambda_init),
        grid=(b, DA_HEADS),
        in_specs=[pl.BlockSpec((None, s, hw), lambda i, h: (i, 0, OFF_DQ // hw + h)),
                  pl.BlockSpec((None, s, hw), lambda i, h: (i, 0, OFF_DK // hw + h)),
                  pl.BlockSpec((None, s, hw), lambda i, h: (i, 0, OFF_DV // hw + h)),
                  small, small, small, small,
                  pl.BlockSpec((1, hw), lambda i, h: (0, 0))],
        out_specs=pl.BlockSpec((None, s, hw), lambda i, h: (i, 0, h)),
        out_shape=jax.ShapeDtypeStruct((b, s, DA_V_W), BF16),
        compiler_params=_params(2),
        name="diffattn",
    )(h3, h3, h3, *lams, norm_g)


def _layer_norm(y, g, b):
    mu = jnp.mean(y, axis=-1, keepdims=True)
    d = y - mu
    var = jnp.mean(d * d, axis=-1, keepdims=True)
    return d * lax.rsqrt(var + LN_EPS) * g + b


def _merge_kernel(x_ref, oa_ref, ob_ref, oc_ref, wg_ref, bg_ref, wa_ref, wb_ref, wc_ref, wo_ref, g_ref, b_ref, o_ref):
    xf = x_ref[...]
    xb = xf.astype(BF16)
    merged = None
    for n, (br_ref, w_ref) in enumerate(((oa_ref, wa_ref), (ob_ref, wb_ref), (oc_ref, wc_ref))):
        cols = slice(n * D_MODEL, (n + 1) * D_MODEL)
        z = _dot(xb, wg_ref[:, cols]) + bg_ref[:, cols]
        term = _dot(br_ref[...], w_ref[...]) * (1.0 / (1.0 + jnp.exp(-z)))
        merged = term if merged is None else merged + term
    mix = _dot(merged.astype(BF16), wo_ref[...])
    o_ref[...] = _layer_norm(DEEPNORM_ALPHA * xf + mix, g_ref[...], b_ref[...])


def _merge(x2, oa, ob, oc, wg, bg, wa, wb, wc, wo, g, b, tm):
    t = x2.shape[0]
    row = lambda w: pl.BlockSpec((tm, w), lambda i: (i, 0))
    return pl.pallas_call(
        _merge_kernel,
        grid=(t // tm,),
        in_specs=[row(D_MODEL), row(RET_V_W), row(CA_W), row(DA_V_W),
                  _const_spec((D_MODEL, GATE_W)), _const_spec((1, GATE_W)),
                  _const_spec((RET_V_W, D_MODEL)), _const_spec((CA_W, D_MODEL)), _const_spec((DA_V_W, D_MODEL)),
                  _const_spec((D_MODEL, D_MODEL)), _const_spec((1, D_MODEL)), _const_spec((1, D_MODEL))],
        out_specs=row(D_MODEL),
        out_shape=jax.ShapeDtypeStruct((t, D_MODEL), F32),
        compiler_params=_params(1),
        name="merge",
    )(x2, oa, ob, oc, wg, bg, wa, wb, wc, wo, g, b)


FFN_CHUNK = 256


def _ffn_kernel(x_ref, wi_ref, wo_ref, g_ref, b_ref, o_ref, act_ref):
    xf = x_ref[...]
    xb = xf.astype(BF16)
    for c in range(0, FFN_HIDDEN, FFN_CHUNK):
        ug = _dot(xb, wi_ref[:, c:c + FFN_CHUNK])
        uu = _dot(xb, wi_ref[:, FFN_HIDDEN + c:FFN_HIDDEN + c + FFN_CHUNK])
        act_ref[:, c:c + FFN_CHUNK] = (ug / (1.0 + jnp.exp(-ug)) * uu).astype(BF16)
    ffn = _dot(act_ref[...], wo_ref[...])
    o_ref[...] = _layer_norm(DEEPNORM_ALPHA * xf + ffn, g_ref[...], b_ref[...])


def _ffn(x2, wi, wo, g, b, tm):
    t = x2.shape[0]
    return pl.pallas_call(
        _ffn_kernel,
        grid=(t // tm,),
        in_specs=[pl.BlockSpec((tm, D_MODEL), lambda i: (i, 0)),
                  _const_spec((D_MODEL, 2 * FFN_HIDDEN)), _const_spec((FFN_HIDDEN, D_MODEL)),
                  _const_spec((1, D_MODEL)), _const_spec((1, D_MODEL))],
        out_specs=pl.BlockSpec((tm, D_MODEL), lambda i: (i, 0)),
        out_shape=jax.ShapeDtypeStruct((t, D_MODEL), F32),
        scratch_shapes=[pltpu.VMEM((tm, FFN_HIDDEN), BF16)],
        compiler_params=_params(1),
        name="ffn",
    )(x2, wi, wo, g, b)


def _rotary_tables(seq):
    pos = jnp.arange(seq, dtype=F32)[:, None]

    def cs(d):
        inv_freq = ROPE_THETA ** (-jnp.arange(0, d, 2, dtype=F32) / d)
        ang = pos * inv_freq[None, :]
        return jnp.cos(ang), jnp.sin(ang)

    rc, rs = cs(RET_DK)
    rcos = jnp.concatenate([rc, rc], axis=1)
    rsin = jnp.concatenate([-rs, rs], axis=1)
    dc, ds = cs(DA_DH)
    z = jnp.zeros_like(ds)
    dcos = jnp.concatenate([dc, dc, dc, dc], axis=1)
    dsina = jnp.concatenate([-ds, z, -ds, z], axis=1)
    dsinb = jnp.concatenate([z, ds, z, ds], axis=1)
    return rcos, rsin, dcos, dsina, dsinb


def kernel(x, w_in, ret_norm_g, ca_rel_bias, da_lambda_q1, da_lambda_k1, da_lambda_q2, da_lambda_k2, da_norm_g, w_branch_a, w_branch_b, w_branch_c, b_merge, w_out, ln1_g, ln1_b, w_ffn_in, w_ffn_out, ln2_g, ln2_b):
    b, s, d = x.shape
    t = b * s
    tm = 512
    rot_tabs = _rotary_tables(s)
    ret_tabs = _retention_tables()
    row = lambda a: a.reshape(1, -1).astype(F32)

    x2 = x.reshape(t, d)
    for l in range(DEPTH):
        lambda_init = 0.8 - 0.6 * math.exp(-0.3 * l)
        w_mix = w_in[l, :, :MIX_W].astype(BF16)
        w_gate = w_in[l, :, MIX_W:].astype(BF16)

        h = _inproj(x2, w_mix, rot_tabs, s, tm)
        h3 = h.reshape(b, s, MIX_W)
        o_a = _retention(h3, ret_tabs, row(ret_norm_g[l]))
        o_b = _chunkattn(h3, _chunkattn_bias(ca_rel_bias[l]))
        o_c = _diffattn(h3, (row(da_lambda_q1[l]), row(da_lambda_k1[l]), row(da_lambda_q2[l]), row(da_lambda_k2[l])),
                        row(da_norm_g[l]), lambda_init)
        x2 = _merge(x2, o_a.reshape(t, RET_V_W), o_b.reshape(t, CA_W), o_c.reshape(t, DA_V_W),
                    w_gate, row(b_merge[l]), w_branch_a[l].astype(BF16), w_branch_b[l].astype(BF16),
                    w_branch_c[l].astype(BF16), w_out[l].astype(BF16), row(ln1_g[l]), row(ln1_b[l]), tm)
        x2 = _ffn(x2, w_ffn_in[l].astype(BF16), w_ffn_out[l].astype(BF16), row(ln2_g[l]), row(ln2_b[l]), tm)
    return x2.reshape(b, s, d)
```

```python
import functools
import math

import jax
import jax.numpy as jnp
import numpy as np
from jax import lax
from jax.experimental import pallas as pl
from jax.experimental.pallas import tpu as pltpu

D_MODEL = 1024
DEPTH = 4
CHUNK = 64
RET_HEADS, RET_DK, RET_DV = 4, 128, 256
CA_HEADS, CA_DH, CA_LEFT_CHUNKS, REL_CLIP = 8, 64, 8, 256
DA_HEADS, DA_DH = 4, 64
FFN_HIDDEN = -(-8 * D_MODEL // (3 * 256)) * 256
ROPE_THETA = 10000.0
LN_EPS = 1e-5
NEG_INF = -1e30
DEEPNORM_ALPHA = (2.0 * DEPTH) ** 0.25

RET_QK_W = RET_HEADS * RET_DK
RET_V_W = RET_HEADS * RET_DV
CA_W = CA_HEADS * CA_DH
DA_QK_W = 2 * DA_HEADS * DA_DH
DA_V_W = DA_HEADS * 2 * DA_DH
OFF_RQ = 0
OFF_RK = OFF_RQ + RET_QK_W
OFF_RV = OFF_RK + RET_QK_W
OFF_RG = OFF_RV + RET_V_W
OFF_CQ = OFF_RG + RET_V_W
OFF_CK = OFF_CQ + CA_W
OFF_CV = OFF_CK + CA_W
OFF_DQ = OFF_CV + CA_W
OFF_DK = OFF_DQ + DA_QK_W
OFF_DV = OFF_DK + DA_QK_W
MIX_W = OFF_DV + DA_V_W
GATE_W = 3 * D_MODEL

LANES = 128
VMEM_LIMIT = 56 * 1024 * 1024

RET_BLOCK = 256
CA_QBLOCK = 256
CA_HALF = CA_QBLOCK // 2
CA_PAD = CA_LEFT_CHUNKS * CHUNK
CA_WIN = CA_PAD + CA_QBLOCK
CA_HWIN = CA_PAD + CA_HALF
DA_BLOCK = 256

BF16 = jnp.bfloat16
F32 = jnp.float32


def _dot(a, b):
    return jnp.dot(a, b, preferred_element_type=F32)


def _dot_nt(a, b):
    return lax.dot_general(a, b, (((1,), (1,)), ((), ())), preferred_element_type=F32)


def _dot_tn(a, b):
    return lax.dot_general(a, b, (((0,), (0,)), ((), ())), preferred_element_type=F32)


def _const_spec(shape):
    zeros = (0,) * len(shape)
    return pl.BlockSpec(shape, lambda *_: zeros, pipeline_mode=pl.Buffered(1))


def _params(n_axes):
    return pltpu.CompilerParams(dimension_semantics=("arbitrary",) * n_axes, vmem_limit_bytes=VMEM_LIMIT)


def _inproj_kernel(x_ref, w_ref, rcos_ref, rsin_ref, dcos_ref, dsina_ref, dsinb_ref, o_ref):
    xb = x_ref[...].astype(BF16)

    def proj(off, width):
        return _dot(xb, w_ref[:, off:off + width])

    def ret_rotary(a, scale):
        outs = []
        for h in range(RET_HEADS):
            ah = a[:, h * RET_DK:(h + 1) * RET_DK]
            r = ah * rcos_ref[...] + pltpu.roll(ah, RET_DK // 2, 1) * rsin_ref[...]
            outs.append(r * scale if scale != 1.0 else r)
        return jnp.concatenate(outs, axis=1)

    def da_rotary(a, scale):
        outs = []
        for g in range(DA_QK_W // LANES):
            ag = a[:, g * LANES:(g + 1) * LANES]
            r = (ag * dcos_ref[...] + pltpu.roll(ag, LANES - DA_DH // 2, 1) * dsina_ref[...]
                 + pltpu.roll(ag, DA_DH // 2, 1) * dsinb_ref[...])
            outs.append(r * scale if scale != 1.0 else r)
        return jnp.concatenate(outs, axis=1)

    o_ref[:, OFF_RQ:OFF_RQ + RET_QK_W] = ret_rotary(proj(OFF_RQ, RET_QK_W), 1.0).astype(BF16)
    o_ref[:, OFF_RK:OFF_RK + RET_QK_W] = ret_rotary(proj(OFF_RK, RET_QK_W), RET_DK ** -0.5).astype(BF16)
    for off in range(OFF_RV, OFF_CQ, 512):
        o_ref[:, off:off + 512] = proj(off, 512).astype(BF16)
    o_ref[:, OFF_CQ:OFF_CQ + CA_W] = (proj(OFF_CQ, CA_W) * (CA_DH ** -0.5)).astype(BF16)
    o_ref[:, OFF_CK:OFF_CK + CA_W] = proj(OFF_CK, CA_W).astype(BF16)
    o_ref[:, OFF_CV:OFF_CV + CA_W] = proj(OFF_CV, CA_W).astype(BF16)
    o_ref[:, OFF_DQ:OFF_DQ + DA_QK_W] = da_rotary(proj(OFF_DQ, DA_QK_W), DA_DH ** -0.5).astype(BF16)
    o_ref[:, OFF_DK:OFF_DK + DA_QK_W] = da_rotary(proj(OFF_DK, DA_QK_W), 1.0).astype(BF16)
    o_ref[:, OFF_DV:OFF_DV + DA_V_W] = proj(OFF_DV, DA_V_W).astype(BF16)


def _inproj(x2, w_mix, tabs, seq, tm):
    t = x2.shape[0]
    pos_blocks = seq // tm
    tab_spec = pl.BlockSpec((tm, LANES), lambda i: (i % pos_blocks, 0))
    return pl.pallas_call(
        _inproj_kernel,
        grid=(t // tm,),
        in_specs=[pl.BlockSpec((tm, D_MODEL), lambda i: (i, 0)),
                  _const_spec((D_MODEL, MIX_W)),
                  tab_spec, tab_spec, tab_spec, tab_spec, tab_spec],
        out_specs=pl.BlockSpec((tm, MIX_W), lambda i: (i, 0)),
        out_shape=jax.ShapeDtypeStruct((t, MIX_W), BF16),
        compiler_params=_params(1),
        name="inproj",
    )(x2, w_mix, *tabs)


def _retention_kernel(q_ref, k_ref, v_ref, g_ref, dmat_ref, qdec_ref, kdec_ref, cdec_ref, ng_ref, o_ref, state_ref):
    n_blocks = q_ref.shape[0] // RET_BLOCK
    state_ref[...] = jnp.zeros_like(state_ref)

    def step(c, carry):
        r0 = pl.multiple_of(c * RET_BLOCK, RET_BLOCK)
        rows = pl.ds(r0, RET_BLOCK)
        q = q_ref[rows, :]
        k = k_ref[rows, :]
        v = v_ref[rows, :]
        s = _dot_nt(q, k) * dmat_ref[...]
        o = _dot(s.astype(BF16), v)
        state = state_ref[...]
        qd = (q.astype(F32) * qdec_ref[...]).astype(BF16)
        o = o + _dot(qd, state.astype(BF16))
        kd = (k.astype(F32) * kdec_ref[...]).astype(BF16)
        state_ref[...] = state * cdec_ref[0:1, :] + _dot_tn(kd, v)
        ms = jnp.mean(o * o, axis=-1, keepdims=True)
        y = o * lax.rsqrt(ms + LN_EPS) * ng_ref[...]
        gate = g_ref[rows, :].astype(F32)
        o_ref[rows, :] = (gate / (1.0 + jnp.exp(-gate)) * y).astype(BF16)
        return carry

    lax.fori_loop(0, n_blocks, step, 0)


def _retention(h3, tabs, norm_g):
    b, s, _ = h3.shape
    dmat, qdec, kdec, cdec = tabs
    return pl.pallas_call(
        _retention_kernel,
        grid=(b, RET_HEADS),
        in_specs=[pl.BlockSpec((None, s, RET_DK), lambda i, h: (i, 0, OFF_RQ // RET_DK + h)),
                  pl.BlockSpec((None, s, RET_DK), lambda i, h: (i, 0, OFF_RK // RET_DK + h)),
                  pl.BlockSpec((None, s, RET_DV), lambda i, h: (i, 0, OFF_RV // RET_DV + h)),
                  pl.BlockSpec((None, s, RET_DV), lambda i, h: (i, 0, OFF_RG // RET_DV + h)),
                  pl.BlockSpec((None, RET_BLOCK, RET_BLOCK), lambda i, h: (h, 0, 0)),
                  pl.BlockSpec((None, RET_BLOCK, RET_DK), lambda i, h: (h, 0, 0)),
                  pl.BlockSpec((None, RET_BLOCK, RET_DK), lambda i, h: (h, 0, 0)),
                  pl.BlockSpec((None, 8, RET_DV), lambda i, h: (h, 0, 0)),
                  pl.BlockSpec((1, RET_DV), lambda i, h: (0, 0))],
        out_specs=pl.BlockSpec((None, s, RET_DV), lambda i, h: (i, 0, h)),
        out_shape=jax.ShapeDtypeStruct((b, s, RET_V_W), BF16),
        scratch_shapes=[pltpu.VMEM((RET_DK, RET_DV), F32)],
        compiler_params=_params(2),
        name="retention",
    )(h3, h3, h3, h3, dmat, qdec, kdec, cdec, norm_g)


def _retention_tables():
    h = np.arange(RET_HEADS, dtype=np.float64)
    log_gamma = np.log1p(-np.exp2(-5.0 - h))
    n = np.arange(RET_BLOCK)
    diff = (n[:, None] - n[None, :]).astype(np.float64)
    cn, cm = n[:, None] // CHUNK, n[None, :] // CHUNK
    expo = np.where(cm == cn, np.abs(diff), diff)
    dmat = np.where(cm <= cn, np.exp(log_gamma[:, None, None] * expo[None]), 0.0)
    qdec = np.exp(log_gamma[:, None] * (n[None, :] + 1.0))
    kdec = np.exp(log_gamma[:, None] * (RET_BLOCK - 1.0 - n[None, :]))
    cdec = np.exp(log_gamma * RET_BLOCK)
    qdec = np.broadcast_to(qdec[:, :, None], (RET_HEADS, RET_BLOCK, RET_DK))
    kdec = np.broadcast_to(kdec[:, :, None], (RET_HEADS, RET_BLOCK, RET_DK))
    cdec = np.broadcast_to(cdec[:, None, None], (RET_HEADS, 8, RET_DV))
    return tuple(jnp.asarray(a, F32) for a in (dmat, qdec, kdec, cdec))


def _chunkattn_kernel(q_ref, k_ref, v_ref, bias_ref, o_ref, kpad_ref, vt_ref, s_ref, p_ref, ot_ref):
    s_len = q_ref.shape[0]
    n_blocks = s_len // CA_QBLOCK
    n_pad = CA_PAD // CA_QBLOCK
    kpad_ref[0:CA_PAD, :] = jnp.zeros((CA_PAD, CA_W), BF16)
    kpad_ref[CA_PAD:, :] = k_ref[...]
    for c in range(n_pad):
        vt_ref[c] = jnp.zeros((CA_W, CA_QBLOCK), BF16)
    for c in range(n_blocks):
        vt_ref[n_pad + c] = v_ref[c * CA_QBLOCK:(c + 1) * CA_QBLOCK, :].astype(F32).T.astype(BF16)
    p_ref[:, CA_HWIN:CA_WIN, 0:CA_HALF] = jnp.zeros((2, CA_WIN - CA_HWIN, CA_HALF), BF16)
    p_ref[:, 0:CA_HALF, CA_HALF:CA_QBLOCK] = jnp.zeros((2, CA_HALF, CA_HALF), BF16)
    lane = lax.broadcasted_iota(jnp.int32, (CA_QBLOCK, LANES), 1)
    head_lanes = (lane < CA_DH, lane >= CA_DH)
    win_row = lax.broadcasted_iota(jnp.int32, (CA_HWIN, CA_HALF), 0)

    def qblock(i, first_key):
        r0 = i * CA_QBLOCK if first_key is not None else pl.multiple_of(i * CA_QBLOCK, CA_QBLOCK)
        win = pl.ds(r0, CA_WIN)
        qrows = pl.ds(r0, CA_QBLOCK)

        def scores(h):
            pair = slice((h // 2) * LANES, (h // 2 + 1) * LANES)
            q = q_ref[qrows, pair]
            qz = jnp.where(head_lanes[h % 2], q, jnp.zeros_like(q))
            s_ref[h % 2] = _dot_nt(kpad_ref[win, pair], qz)

        def attend(h):
            inv_l = []
            for half in range(2):
                rows = slice(half * CA_HALF, half * CA_HALF + CA_HWIN)
                cols = slice(half * CA_HALF, (half + 1) * CA_HALF)
                s = s_ref[h % 2, rows, cols] + bias_ref[h]
                if first_key is not None:
                    s = jnp.where(win_row >= -(first_key + half * CA_HALF), s, NEG_INF)
                m = jnp.max(s, axis=0, keepdims=True)
                p = jnp.exp(s - m)
                inv_l.append(1.0 / jnp.sum(p, axis=0, keepdims=True))
                p_ref[h % 2, rows, cols] = p.astype(BF16)
            o_t = None
            for c in range(CA_WIN // CA_QBLOCK):
                part = _dot(vt_ref[i + c, h * CA_DH:(h + 1) * CA_DH, :],
                            p_ref[h % 2, c * CA_QBLOCK:(c + 1) * CA_QBLOCK, :])
                o_t = part if o_t is None else o_t + part
            ot_ref[h * CA_DH:(h + 1) * CA_DH, :] = o_t * jnp.concatenate(inv_l, axis=1)

        scores(0)
        for h in range(CA_HEADS):
            if h + 1 < CA_HEADS:
                scores(h + 1)
            attend(h)
        o_ref[qrows, :] = ot_ref[...].T.astype(BF16)

    for i in range(n_pad):
        qblock(i, i * CA_QBLOCK - CA_PAD)

    def body(i, carry):
        qblock(i, None)
        return carry

    lax.fori_loop(n_pad, n_blocks, body, 0)


def _chunkattn(h3, bias):
    b, s, _ = h3.shape
    n_blocks = s // CA_QBLOCK
    return pl.pallas_call(
        _chunkattn_kernel,
        grid=(b,),
        in_specs=[pl.BlockSpec((None, s, CA_W), lambda i: (i, 0, OFF_CQ // CA_W)),
                  pl.BlockSpec((None, s, CA_W), lambda i: (i, 0, OFF_CK // CA_W)),
                  pl.BlockSpec((None, s, CA_W), lambda i: (i, 0, OFF_CV // CA_W)),
                  _const_spec((CA_HEADS, CA_HWIN, CA_HALF))],
        out_specs=pl.BlockSpec((None, s, CA_W), lambda i: (i, 0, 0)),
        out_shape=jax.ShapeDtypeStruct((b, s, CA_W), BF16),
        scratch_shapes=[pltpu.VMEM((s + CA_PAD, CA_W), BF16),
                        pltpu.VMEM((n_blocks + CA_PAD // CA_QBLOCK, CA_W, CA_QBLOCK), BF16),
                        pltpu.VMEM((2, CA_WIN, CA_QBLOCK), F32),
                        pltpu.VMEM((2, CA_WIN, CA_QBLOCK), BF16),
                        pltpu.VMEM((CA_W, CA_QBLOCK), F32)],
        compiler_params=_params(1),
        name="chunkattn",
    )(h3, h3, h3, bias)


def _chunkattn_bias(rel_bias):
    j = np.arange(CA_HWIN)[:, None]
    r = np.arange(CA_HALF)[None, :]
    rel = r + CA_PAD - j
    idx = np.clip(rel, -REL_CLIP, REL_CLIP) + REL_CLIP
    qc, kc = r // CHUNK, j // CHUNK
    band = (kc >= qc) & (kc <= qc + CA_LEFT_CHUNKS)
    return jnp.where(jnp.asarray(band)[None], rel_bias[:, idx].astype(F32), NEG_INF)


def _diffattn_kernel(lambda_init, q_ref, k_ref, v_ref, lq1_ref, lk1_ref, lq2_ref, lk2_ref, ng_ref, o_ref,
                     vt_ref, s_ref, p_ref):
    s_len = q_ref.shape[0]
    n_blocks = s_len // DA_BLOCK
    hw = 2 * DA_DH
    lam = (jnp.exp(jnp.sum(lq1_ref[...] * lk1_ref[...], axis=-1, keepdims=True))
           - jnp.exp(jnp.sum(lq2_ref[...] * lk2_ref[...], axis=-1, keepdims=True)) + lambda_init)
    for c in range(n_blocks):
        blk = slice(c * DA_BLOCK, (c + 1) * DA_BLOCK)
        vt_ref[:, blk] = v_ref[blk, :].astype(F32).T.astype(BF16)
    key_chunk = lax.broadcasted_iota(jnp.int32, (DA_BLOCK, DA_BLOCK), 0) // CHUNK
    query_chunk = lax.broadcasted_iota(jnp.int32, (DA_BLOCK, DA_BLOCK), 1) // CHUNK
    diag_allowed = key_chunk <= query_chunk
    lane = lax.broadcasted_iota(jnp.int32, (DA_BLOCK, hw), 1)
    head_lanes = (lane < DA_DH, lane >= DA_DH)

    def scores(i):
        n = (i + 1) * DA_BLOCK
        q = q_ref[i * DA_BLOCK:(i + 1) * DA_BLOCK, :]
        for t in range(2):
            qz = jnp.where(head_lanes[t], q, jnp.zeros_like(q))
            s_ref[i % 2, t, 0:n, :] = _dot_nt(k_ref[0:n, :], qz)

    def attend(i):
        n = (i + 1) * DA_BLOCK

        def chunk(t, c):
            s = s_ref[i % 2, t, c * DA_BLOCK:(c + 1) * DA_BLOCK, :]
            return jnp.where(diag_allowed, s, NEG_INF) if c == i else s

        heads = []
        for t in range(2):
            m = None
            for c in range(i + 1):
                mc = jnp.max(chunk(t, c), axis=0, keepdims=True)
                m = mc if m is None else jnp.maximum(m, mc)
            l = None
            for c in range(i + 1):
                p = jnp.exp(chunk(t, c) - m)
                lc = jnp.sum(p, axis=0, keepdims=True)
                l = lc if l is None else l + lc
                p_ref[t, c * DA_BLOCK:(c + 1) * DA_BLOCK, :] = p.astype(BF16)
            heads.append(_dot(vt_ref[:, 0:n], p_ref[t, 0:n, :]) * (1.0 / l))
        o_t = heads[0] - lam * heads[1]
        ms = jnp.mean(o_t * o_t, axis=0, keepdims=True)
        y = (o_t * lax.rsqrt(ms + LN_EPS)).T * ng_ref[...] * (1.0 - lambda_init)
        o_ref[i * DA_BLOCK:(i + 1) * DA_BLOCK, :] = y.astype(BF16)

    scores(0)
    for i in range(n_blocks):
        if i + 1 < n_blocks:
            scores(i + 1)
        attend(i)


def _diffattn(h3, lams, norm_g, lambda_init):
    b, s, _ = h3.shape
    hw = 2 * DA_DH
    small = pl.BlockSpec((1, DA_DH), lambda i, h: (0, 0))
    return pl.pallas_call(
        functools.partial(_diffattn_kernel, lambda_init),
        grid=(b, DA_HEADS),
        in_specs=[pl.BlockSpec((None, s, hw), lambda i, h: (i, 0, OFF_DQ // hw + h)),
                  pl.BlockSpec((None, s, hw), lambda i, h: (i, 0, OFF_DK // hw + h)),
                  pl.BlockSpec((None, s, hw), lambda i, h: (i, 0, OFF_DV // hw + h)),
                  small, small, small, small,
                  pl.BlockSpec((1, hw), lambda i, h: (0, 0))],
        out_specs=pl.BlockSpec((None, s, hw), lambda i, h: (i, 0, h)),
        out_shape=jax.ShapeDtypeStruct((b, s, DA_V_W), BF16),
        scratch_shapes=[pltpu.VMEM((hw, s), BF16),
                        pltpu.VMEM((2, 2, s, DA_BLOCK), F32),
                        pltpu.VMEM((2, s, DA_BLOCK), BF16)],
        compiler_params=_params(2),
        name="diffattn",
    )(h3, h3, h3, *lams, norm_g)


def _layer_norm(y, g, b):
    mu = jnp.mean(y, axis=-1, keepdims=True)
    d = y - mu
    var = jnp.mean(d * d, axis=-1, keepdims=True)
    return d * lax.rsqrt(var + LN_EPS) * g + b


def _merge_kernel(x_ref, oa_ref, ob_ref, oc_ref, wg_ref, bg_ref, wa_ref, wb_ref, wc_ref, wo_ref, g_ref, b_ref, o_ref):
    xf = x_ref[...]
    xb = xf.astype(BF16)
    merged = None
    for n, (br_ref, w_ref) in enumerate(((oa_ref, wa_ref), (ob_ref, wb_ref), (oc_ref, wc_ref))):
        cols = slice(n * D_MODEL, (n + 1) * D_MODEL)
        z = _dot(xb, wg_ref[:, cols]) + bg_ref[:, cols]
        term = _dot(br_ref[...], w_ref[...]) * (1.0 / (1.0 + jnp.exp(-z)))
        merged = term if merged is None else merged + term
    mix = _dot(merged.astype(BF16), wo_ref[...])
    o_ref[...] = _layer_norm(DEEPNORM_ALPHA * xf + mix, g_ref[...], b_ref[...])


def _merge(x2, oa, ob, oc, wg, bg, wa, wb, wc, wo, g, b, tm):
    t = x2.shape[0]
    row = lambda w: pl.BlockSpec((tm, w), lambda i: (i, 0))
    return pl.pallas_call(
        _merge_kernel,
        grid=(t // tm,),
        in_specs=[row(D_MODEL), row(RET_V_W), row(CA_W), row(DA_V_W),
                  _const_spec((D_MODEL, GATE_W)), _const_spec((1, GATE_W)),
                  _const_spec((RET_V_W, D_MODEL)), _const_spec((CA_W, D_MODEL)), _const_spec((DA_V_W, D_MODEL)),
                  _const_spec((D_MODEL, D_MODEL)), _const_spec((1, D_MODEL)), _const_spec((1, D_MODEL))],
        out_specs=row(D_MODEL),
        out_shape=jax.ShapeDtypeStruct((t, D_MODEL), F32),
        compiler_params=_params(1),
        name="merge",
    )(x2, oa, ob, oc, wg, bg, wa, wb, wc, wo, g, b)


FFN_CHUNK = 256


def _ffn_kernel(x_ref, wi_ref, wo_ref, g_ref, b_ref, o_ref, act_ref):
    xf = x_ref[...]
    xb = xf.astype(BF16)
    for c in range(0, FFN_HIDDEN, FFN_CHUNK):
        ug = _dot(xb, wi_ref[:, c:c + FFN_CHUNK])
        uu = _dot(xb, wi_ref[:, FFN_HIDDEN + c:FFN_HIDDEN + c + FFN_CHUNK])
        act_ref[:, c:c + FFN_CHUNK] = (ug / (1.0 + jnp.exp(-ug)) * uu).astype(BF16)
    ffn = _dot(act_ref[...], wo_ref[...])
    o_ref[...] = _layer_norm(DEEPNORM_ALPHA * xf + ffn, g_ref[...], b_ref[...])


def _ffn(x2, wi, wo, g, b, tm):
    t = x2.shape[0]
    return pl.pallas_call(
        _ffn_kernel,
        grid=(t // tm,),
        in_specs=[pl.BlockSpec((tm, D_MODEL), lambda i: (i, 0)),
                  _const_spec((D_MODEL, 2 * FFN_HIDDEN)), _const_spec((FFN_HIDDEN, D_MODEL)),
                  _const_spec((1, D_MODEL)), _const_spec((1, D_MODEL))],
        out_specs=pl.BlockSpec((tm, D_MODEL), lambda i: (i, 0)),
        out_shape=jax.ShapeDtypeStruct((t, D_MODEL), F32),
        scratch_shapes=[pltpu.VMEM((tm, FFN_HIDDEN), BF16)],
        compiler_params=_params(1),
        name="ffn",
    )(x2, wi, wo, g, b)


def _rotary_tables(seq):
    pos = jnp.arange(seq, dtype=F32)[:, None]

    def cs(d):
        inv_freq = ROPE_THETA ** (-jnp.arange(0, d, 2, dtype=F32) / d)
        ang = pos * inv_freq[None, :]
        return jnp.cos(ang), jnp.sin(ang)

    rc, rs = cs(RET_DK)
    rcos = jnp.concatenate([rc, rc], axis=1)
    rsin = jnp.concatenate([-rs, rs], axis=1)
    dc, ds = cs(DA_DH)
    z = jnp.zeros_like(ds)
    dcos = jnp.concatenate([dc, dc, dc, dc], axis=1)
    dsina = jnp.concatenate([-ds, z, -ds, z], axis=1)
    dsinb = jnp.concatenate([z, ds, z, ds], axis=1)
    return rcos, rsin, dcos, dsina, dsinb


def kernel(x, w_in, ret_norm_g, ca_rel_bias, da_lambda_q1, da_lambda_k1, da_lambda_q2, da_lambda_k2, da_norm_g, w_branch_a, w_branch_b, w_branch_c, b_merge, w_out, ln1_g, ln1_b, w_ffn_in, w_ffn_out, ln2_g, ln2_b):
    b, s, d = x.shape
    t = b * s
    tm = 512
    rot_tabs = _rotary_tables(s)
    ret_tabs = _retention_tables()
    row = lambda a: a.reshape(1, -1).astype(F32)

    x2 = x.reshape(t, d)
    for l in range(DEPTH):
        lambda_init = 0.8 - 0.6 * math.exp(-0.3 * l)
        w_mix = w_in[l, :, :MIX_W].astype(BF16)
        w_gate = w_in[l, :, MIX_W:].astype(BF16)

        h = _inproj(x2, w_mix, rot_tabs, s, tm)
        h3 = h.reshape(b, s, MIX_W)
        o_a = _retention(h3, ret_tabs, row(ret_norm_g[l]))
        o_b = _chunkattn(h3, _chunkattn_bias(ca_rel_bias[l]))
        o_c = _diffattn(h3, (row(da_lambda_q1[l]), row(da_lambda_k1[l]), row(da_lambda_q2[l]), row(da_lambda_k2[l])),
                        row(da_norm_g[l]), lambda_init)
        x2 = _merge(x2, o_a.reshape(t, RET_V_W), o_b.reshape(t, CA_W), o_c.reshape(t, DA_V_W),
                    w_gate, row(b_merge[l]), w_branch_a[l].astype(BF16), w_branch_b[l].astype(BF16),
                    w_branch_c[l].astype(BF16), w_out[l].astype(BF16), row(ln1_g[l]), row(ln1_b[l]), tm)
        x2 = _ffn(x2, w_ffn_in[l].astype(BF16), w_ffn_out[l].astype(BF16), row(ln2_g[l]), row(ln2_b[l]), tm)
    return x2.reshape(b, s, d)
```

```python
import functools
import math

import jax
import jax.numpy as jnp
import numpy as np
from jax import lax
from jax.experimental import pallas as pl
from jax.experimental.pallas import tpu as pltpu

D_MODEL = 1024
DEPTH = 4
CHUNK = 64
RET_HEADS, RET_DK, RET_DV = 4, 128, 256
CA_HEADS, CA_DH, CA_LEFT_CHUNKS, REL_CLIP = 8, 64, 8, 256
DA_HEADS, DA_DH = 4, 64
FFN_HIDDEN = -(-8 * D_MODEL // (3 * 256)) * 256
ROPE_THETA = 10000.0
LN_EPS = 1e-5
NEG_INF = -1e30
DEEPNORM_ALPHA = (2.0 * DEPTH) ** 0.25

RET_QK_W = RET_HEADS * RET_DK
RET_V_W = RET_HEADS * RET_DV
CA_W = CA_HEADS * CA_DH
DA_QK_W = 2 * DA_HEADS * DA_DH
DA_V_W = DA_HEADS * 2 * DA_DH
OFF_RQ = 0
OFF_RK = OFF_RQ + RET_QK_W
OFF_RV = OFF_RK + RET_QK_W
OFF_RG = OFF_RV + RET_V_W
OFF_CQ = OFF_RG + RET_V_W
OFF_CK = OFF_CQ + CA_W
OFF_CV = OFF_CK + CA_W
OFF_DQ = OFF_CV + CA_W
OFF_DK = OFF_DQ + DA_QK_W
OFF_DV = OFF_DK + DA_QK_W
MIX_W = OFF_DV + DA_V_W
GATE_W = 3 * D_MODEL

LOG2E = math.log2(math.e)
LANES = 128
ONES_ROWS = 16
VMEM_LIMIT = 56 * 1024 * 1024

RET_BLOCK = 256
CA_QBLOCK = 256
CA_HALF = CA_QBLOCK // 2
CA_PAD = CA_LEFT_CHUNKS * CHUNK
CA_WIN = CA_PAD + CA_QBLOCK
CA_HWIN = CA_PAD + CA_HALF
CA_DEPTH = 3
CA_SLAB = CA_DH + ONES_ROWS
DA_BLOCK = 256
DA_DEPTH = 3

BF16 = jnp.bfloat16
F32 = jnp.float32


def _dot(a, b):
    return jnp.dot(a, b, preferred_element_type=F32)


def _dot_nt(a, b):
    return lax.dot_general(a, b, (((1,), (1,)), ((), ())), preferred_element_type=F32)


def _dot_tn(a, b):
    return lax.dot_general(a, b, (((0,), (0,)), ((), ())), preferred_element_type=F32)


def _const_spec(shape):
    zeros = (0,) * len(shape)
    return pl.BlockSpec(shape, lambda *_: zeros, pipeline_mode=pl.Buffered(1))


def _params(n_axes):
    return pltpu.CompilerParams(dimension_semantics=("arbitrary",) * n_axes, vmem_limit_bytes=VMEM_LIMIT)


def _inproj_kernel(x_ref, w_ref, rcos_ref, rsin_ref, dcos_ref, dsina_ref, dsinb_ref, o_ref):
    xb = x_ref[...].astype(BF16)

    def proj(off, width):
        return _dot(xb, w_ref[:, off:off + width])

    def ret_rotary(a, scale):
        outs = []
        for h in range(RET_HEADS):
            ah = a[:, h * RET_DK:(h + 1) * RET_DK]
            r = ah * rcos_ref[...] + pltpu.roll(ah, RET_DK // 2, 1) * rsin_ref[...]
            outs.append(r * scale if scale != 1.0 else r)
        return jnp.concatenate(outs, axis=1)

    def da_rotary(a, scale):
        outs = []
        for g in range(DA_QK_W // LANES):
            ag = a[:, g * LANES:(g + 1) * LANES]
            r = (ag * dcos_ref[...] + pltpu.roll(ag, LANES - DA_DH // 2, 1) * dsina_ref[...]
                 + pltpu.roll(ag, DA_DH // 2, 1) * dsinb_ref[...])
            outs.append(r * scale if scale != 1.0 else r)
        return jnp.concatenate(outs, axis=1)

    o_ref[:, OFF_RQ:OFF_RQ + RET_QK_W] = ret_rotary(proj(OFF_RQ, RET_QK_W), 1.0).astype(BF16)
    o_ref[:, OFF_RK:OFF_RK + RET_QK_W] = ret_rotary(proj(OFF_RK, RET_QK_W), RET_DK ** -0.5).astype(BF16)
    for off in range(OFF_RV, OFF_CQ, 512):
        o_ref[:, off:off + 512] = proj(off, 512).astype(BF16)
    o_ref[:, OFF_CQ:OFF_CQ + CA_W] = (proj(OFF_CQ, CA_W) * (CA_DH ** -0.5 * LOG2E)).astype(BF16)
    o_ref[:, OFF_CK:OFF_CK + CA_W] = proj(OFF_CK, CA_W).astype(BF16)
    o_ref[:, OFF_CV:OFF_CV + CA_W] = proj(OFF_CV, CA_W).astype(BF16)
    o_ref[:, OFF_DQ:OFF_DQ + DA_QK_W] = da_rotary(proj(OFF_DQ, DA_QK_W), DA_DH ** -0.5 * LOG2E).astype(BF16)
    o_ref[:, OFF_DK:OFF_DK + DA_QK_W] = da_rotary(proj(OFF_DK, DA_QK_W), 1.0).astype(BF16)
    o_ref[:, OFF_DV:OFF_DV + DA_V_W] = proj(OFF_DV, DA_V_W).astype(BF16)


def _inproj(x2, w_mix, tabs, seq, tm):
    t = x2.shape[0]
    pos_blocks = seq // tm
    tab_spec = pl.BlockSpec((tm, LANES), lambda i: (i % pos_blocks, 0))
    return pl.pallas_call(
        _inproj_kernel,
        grid=(t // tm,),
        in_specs=[pl.BlockSpec((tm, D_MODEL), lambda i: (i, 0)),
                  _const_spec((D_MODEL, MIX_W)),
                  tab_spec, tab_spec, tab_spec, tab_spec, tab_spec],
        out_specs=pl.BlockSpec((tm, MIX_W), lambda i: (i, 0)),
        out_shape=jax.ShapeDtypeStruct((t, MIX_W), BF16),
        compiler_params=_params(1),
        name="inproj",
    )(x2, w_mix, *tabs)


def _retention_kernel(q_ref, k_ref, v_ref, g_ref, dmat_ref, qdec_ref, kdec_ref, cdec_ref, ng_ref, o_ref,
                      state_ref, sv_ref, kv_ref):
    n_blocks = q_ref.shape[0] // RET_BLOCK

    def local(c):
        rows = slice(c * RET_BLOCK, (c + 1) * RET_BLOCK)
        q = q_ref[rows, :]
        k = k_ref[rows, :]
        v = v_ref[rows, :]
        s = _dot_nt(q, k) * dmat_ref[...]
        sv_ref[c % 2] = _dot(s.astype(BF16), v)
        kd = (k.astype(F32) * kdec_ref[...]).astype(BF16)
        kv_ref[c % 2] = _dot_tn(kd, v)

    def finish(c):
        rows = slice(c * RET_BLOCK, (c + 1) * RET_BLOCK)
        o = sv_ref[c % 2]
        if c > 0:
            qd = (q_ref[rows, :].astype(F32) * qdec_ref[...]).astype(BF16)
            o = o + _dot(qd, state_ref[...].astype(BF16))
            state_ref[...] = state_ref[...] * cdec_ref[0:1, :] + kv_ref[c % 2]
        else:
            state_ref[...] = kv_ref[c % 2]
        ms = jnp.mean(o * o, axis=-1, keepdims=True)
        y = o * lax.rsqrt(ms + LN_EPS) * ng_ref[...]
        gate = g_ref[rows, :].astype(F32)
        o_ref[rows, :] = (gate / (1.0 + jnp.exp(-gate)) * y).astype(BF16)

    local(0)
    for c in range(n_blocks):
        if c + 1 < n_blocks:
            local(c + 1)
        finish(c)


def _retention(h3, tabs, norm_g):
    b, s, _ = h3.shape
    dmat, qdec, kdec, cdec = tabs
    return pl.pallas_call(
        _retention_kernel,
        grid=(b, RET_HEADS),
        in_specs=[pl.BlockSpec((None, s, RET_DK), lambda i, h: (i, 0, OFF_RQ // RET_DK + h)),
                  pl.BlockSpec((None, s, RET_DK), lambda i, h: (i, 0, OFF_RK // RET_DK + h)),
                  pl.BlockSpec((None, s, RET_DV), lambda i, h: (i, 0, OFF_RV // RET_DV + h)),
                  pl.BlockSpec((None, s, RET_DV), lambda i, h: (i, 0, OFF_RG // RET_DV + h)),
                  pl.BlockSpec((None, RET_BLOCK, RET_BLOCK), lambda i, h: (h, 0, 0)),
                  pl.BlockSpec((None, RET_BLOCK, RET_DK), lambda i, h: (h, 0, 0)),
                  pl.BlockSpec((None, RET_BLOCK, RET_DK), lambda i, h: (h, 0, 0)),
                  pl.BlockSpec((None, 8, RET_DV), lambda i, h: (h, 0, 0)),
                  pl.BlockSpec((1, RET_DV), lambda i, h: (0, 0))],
        out_specs=pl.BlockSpec((None, s, RET_DV), lambda i, h: (i, 0, h)),
        out_shape=jax.ShapeDtypeStruct((b, s, RET_V_W), BF16),
        scratch_shapes=[pltpu.VMEM((RET_DK, RET_DV), F32),
                        pltpu.VMEM((2, RET_BLOCK, RET_DV), F32),
                        pltpu.VMEM((2, RET_DK, RET_DV), F32)],
        compiler_params=_params(2),
        name="retention",
    )(h3, h3, h3, h3, dmat, qdec, kdec, cdec, norm_g)


def _retention_tables():
    h = np.arange(RET_HEADS, dtype=np.float64)
    log_gamma = np.log1p(-np.exp2(-5.0 - h))
    n = np.arange(RET_BLOCK)
    diff = (n[:, None] - n[None, :]).astype(np.float64)
    cn, cm = n[:, None] // CHUNK, n[None, :] // CHUNK
    expo = np.where(cm == cn, np.abs(diff), diff)
    dmat = np.where(cm <= cn, np.exp(log_gamma[:, None, None] * expo[None]), 0.0)
    qdec = np.exp(log_gamma[:, None] * (n[None, :] + 1.0))
    kdec = np.exp(log_gamma[:, None] * (RET_BLOCK - 1.0 - n[None, :]))
    cdec = np.exp(log_gamma * RET_BLOCK)
    qdec = np.broadcast_to(qdec[:, :, None], (RET_HEADS, RET_BLOCK, RET_DK))
    kdec = np.broadcast_to(kdec[:, :, None], (RET_HEADS, RET_BLOCK, RET_DK))
    cdec = np.broadcast_to(cdec[:, None, None], (RET_HEADS, 8, RET_DV))
    return tuple(jnp.asarray(a, F32) for a in (dmat, qdec, kdec, cdec))


def _chunkattn_kernel(q_ref, k_ref, v_ref, bias_ref, o_ref, kpad_ref, vt_ref, s_ref, p_ref, ot_ref):
    s_len = q_ref.shape[0]
    n_blocks = s_len // CA_QBLOCK
    n_pad = CA_PAD // CA_QBLOCK
    kpad_ref[0:CA_PAD, :] = jnp.zeros((CA_PAD, CA_W), BF16)
    kpad_ref[CA_PAD:, :] = k_ref[...]
    ones = jnp.ones((ONES_ROWS, CA_QBLOCK), BF16)
    for c in range(n_pad + n_blocks):
        if c < n_pad:
            vt = jnp.zeros((CA_W, CA_QBLOCK), BF16)
        else:
            vt = v_ref[(c - n_pad) * CA_QBLOCK:(c - n_pad + 1) * CA_QBLOCK, :].astype(F32).T.astype(BF16)
        for h in range(CA_HEADS):
            vt_ref[c, h * CA_SLAB:h * CA_SLAB + CA_DH, :] = vt[h * CA_DH:(h + 1) * CA_DH, :]
            vt_ref[c, h * CA_SLAB + CA_DH:(h + 1) * CA_SLAB, :] = ones
    p_ref[:, CA_HWIN:CA_WIN, 0:CA_HALF] = jnp.zeros((2, CA_WIN - CA_HWIN, CA_HALF), BF16)
    p_ref[:, 0:CA_HALF, CA_HALF:CA_QBLOCK] = jnp.zeros((2, CA_HALF, CA_HALF), BF16)
    lane = lax.broadcasted_iota(jnp.int32, (CA_QBLOCK, LANES), 1)
    head_lanes = (lane < CA_DH, lane >= CA_DH)
    win_row = lax.broadcasted_iota(jnp.int32, (CA_HWIN, CA_HALF), 0)

    def qblock(i, first_key):
        r0 = i * CA_QBLOCK if first_key is not None else pl.multiple_of(i * CA_QBLOCK, CA_QBLOCK)
        win = pl.ds(r0, CA_WIN)
        qrows = pl.ds(r0, CA_QBLOCK)

        def scores(h):
            pair = slice((h // 2) * LANES, (h // 2 + 1) * LANES)
            q = q_ref[qrows, pair]
            qz = jnp.where(head_lanes[h % 2], q, jnp.zeros_like(q))
            s_ref[h % CA_DEPTH] = _dot_nt(kpad_ref[win, pair], qz)

        def attend(h):
            for half in range(2):
                rows = slice(half * CA_HALF, half * CA_HALF + CA_HWIN)
                cols = slice(half * CA_HALF, (half + 1) * CA_HALF)
                s = s_ref[h % CA_DEPTH, rows, cols] + bias_ref[h]
                if first_key is not None:
                    s = jnp.where(win_row >= -(first_key + half * CA_HALF), s, NEG_INF)
                m = jnp.max(s, axis=0, keepdims=True)
                p_ref[h % 2, rows, cols] = jnp.exp2(s - m).astype(BF16)
            pv = None
            for c in range(CA_WIN // CA_QBLOCK):
                part = _dot(vt_ref[i + c, h * CA_SLAB:(h + 1) * CA_SLAB, :],
                            p_ref[h % 2, c * CA_QBLOCK:(c + 1) * CA_QBLOCK, :])
                pv = part if pv is None else pv + part
            ot_ref[h * CA_DH:(h + 1) * CA_DH, :] = pv[0:CA_DH] * (1.0 / pv[CA_DH:CA_DH + 1])

        for h in range(CA_HEADS + CA_DEPTH - 1):
            if h < CA_HEADS:
                scores(h)
            if h >= CA_DEPTH - 1:
                attend(h - (CA_DEPTH - 1))
        o_ref[qrows, :] = ot_ref[...].T.astype(BF16)

    for i in range(n_pad):
        qblock(i, i * CA_QBLOCK - CA_PAD)

    def body(i, carry):
        qblock(i, None)
        return carry

    lax.fori_loop(n_pad, n_blocks, body, 0)


def _chunkattn(h3, bias):
    b, s, _ = h3.shape
    n_blocks = s // CA_QBLOCK
    return pl.pallas_call(
        _chunkattn_kernel,
        grid=(b,),
        in_specs=[pl.BlockSpec((None, s, CA_W), lambda i: (i, 0, OFF_CQ // CA_W)),
                  pl.BlockSpec((None, s, CA_W), lambda i: (i, 0, OFF_CK // CA_W)),
                  pl.BlockSpec((None, s, CA_W), lambda i: (i, 0, OFF_CV // CA_W)),
                  _const_spec((CA_HEADS, CA_HWIN, CA_HALF))],
        out_specs=pl.BlockSpec((None, s, CA_W), lambda i: (i, 0, 0)),
        out_shape=jax.ShapeDtypeStruct((b, s, CA_W), BF16),
        scratch_shapes=[pltpu.VMEM((s + CA_PAD, CA_W), BF16),
                        pltpu.VMEM((n_blocks + CA_PAD // CA_QBLOCK, CA_HEADS * CA_SLAB, CA_QBLOCK), BF16),
                        pltpu.VMEM((CA_DEPTH, CA_WIN, CA_QBLOCK), F32),
                        pltpu.VMEM((2, CA_WIN, CA_QBLOCK), BF16),
                        pltpu.VMEM((CA_W, CA_QBLOCK), F32)],
        compiler_params=_params(1),
        name="chunkattn",
    )(h3, h3, h3, bias)


def _chunkattn_bias(rel_bias):
    j = np.arange(CA_HWIN)[:, None]
    r = np.arange(CA_HALF)[None, :]
    qc, kc = r // CHUNK, j // CHUNK
    band = (kc >= qc) & (kc <= qc + CA_LEFT_CHUNKS)
    n_diag = CA_HWIN + CA_HALF
    u = np.arange(n_diag)
    idx = np.clip(u - (CA_HWIN - 1) + CA_PAD, -REL_CLIP, REL_CLIP) + REL_CLIP
    diag = rel_bias[:, idx].astype(F32)
    shifted = jnp.tile(diag, (1, CA_HWIN + 1))[:, :CA_HWIN * (n_diag + 1)].reshape(CA_HEADS, CA_HWIN, n_diag + 1)
    table = shifted[:, ::-1, :CA_HALF]
    return jnp.where(jnp.asarray(band)[None], table * LOG2E, NEG_INF)


def _diffattn_kernel(lambda_init, q_ref, k_ref, v_ref, lq1_ref, lk1_ref, lq2_ref, lk2_ref, ng_ref, o_ref,
                     vt_ref, s_ref, p_ref):
    s_len = q_ref.shape[0]
    n_blocks = s_len // DA_BLOCK
    hw = 2 * DA_DH
    lam = (jnp.exp(jnp.sum(lq1_ref[...] * lk1_ref[...], axis=-1, keepdims=True))
           - jnp.exp(jnp.sum(lq2_ref[...] * lk2_ref[...], axis=-1, keepdims=True)) + lambda_init)
    for c in range(n_blocks):
        blk = slice(c * DA_BLOCK, (c + 1) * DA_BLOCK)
        vt_ref[0:hw, blk] = v_ref[blk, :].astype(F32).T.astype(BF16)
    vt_ref[hw:, :] = jnp.ones((ONES_ROWS, s_len), BF16)
    key_chunk = lax.broadcasted_iota(jnp.int32, (DA_BLOCK, DA_BLOCK), 0) // CHUNK
    query_chunk = lax.broadcasted_iota(jnp.int32, (DA_BLOCK, DA_BLOCK), 1) // CHUNK
    diag_allowed = key_chunk <= query_chunk
    lane = lax.broadcasted_iota(jnp.int32, (DA_BLOCK, hw), 1)
    head_lanes = (lane < DA_DH, lane >= DA_DH)

    def scores(i, buf):
        n = (i + 1) * DA_BLOCK
        q = q_ref[i * DA_BLOCK:(i + 1) * DA_BLOCK, :]
        for t in range(2):
            qz = jnp.where(head_lanes[t], q, jnp.zeros_like(q))
            s_ref[buf, t, 0:n, :] = _dot_nt(k_ref[0:n, :], qz)

    def attend(i, buf):
        n = (i + 1) * DA_BLOCK

        def chunk(t, c):
            s = s_ref[buf, t, c * DA_BLOCK:(c + 1) * DA_BLOCK, :]
            return jnp.where(diag_allowed, s, NEG_INF) if c == i else s

        heads = []
        for t in range(2):
            m = None
            for c in range(i + 1):
                mc = jnp.max(chunk(t, c), axis=0, keepdims=True)
                m = mc if m is None else jnp.maximum(m, mc)
            for c in range(i + 1):
                p_ref[t, c * DA_BLOCK:(c + 1) * DA_BLOCK, :] = jnp.exp2(chunk(t, c) - m).astype(BF16)
            pv = _dot(vt_ref[:, 0:n], p_ref[t, 0:n, :])
            heads.append(pv[0:hw] * (1.0 / pv[hw:hw + 1]))
        o_t = heads[0] - lam * heads[1]
        ms = jnp.mean(o_t * o_t, axis=0, keepdims=True)
        y = (o_t * lax.rsqrt(ms + LN_EPS)).T * ng_ref[...] * (1.0 - lambda_init)
        o_ref[i * DA_BLOCK:(i + 1) * DA_BLOCK, :] = y.astype(BF16)

    order = [0] + list(range(n_blocks - 1, 0, -1))
    for pos in range(n_blocks + DA_DEPTH - 1):
        if pos < n_blocks:
            scores(order[pos], pos % DA_DEPTH)
        done = pos - (DA_DEPTH - 1)
        if done >= 0:
            attend(order[done], done % DA_DEPTH)


def _diffattn(h3, lams, norm_g, lambda_init):
    b, s, _ = h3.shape
    hw = 2 * DA_DH
    small = pl.BlockSpec((1, DA_DH), lambda i, h: (0, 0))
    return pl.pallas_call(
        functools.partial(_diffattn_kernel, lambda_init),
        grid=(b, DA_HEADS),
        in_specs=[pl.BlockSpec((None, s, hw), lambda i, h: (i, 0, OFF_DQ // hw + h)),
                  pl.BlockSpec((None, s, hw), lambda i, h: (i, 0, OFF_DK // hw + h)),
                  pl.BlockSpec((None, s, hw), lambda i, h: (i, 0, OFF_DV // hw + h)),
                  small, small, small, small,
                  pl.BlockSpec((1, hw), lambda i, h: (0, 0))],
        out_specs=pl.BlockSpec((None, s, hw), lambda i, h: (i, 0, h)),
        out_shape=jax.ShapeDtypeStruct((b, s, DA_V_W), BF16),
        scratch_shapes=[pltpu.VMEM((hw + ONES_ROWS, s), BF16),
                        pltpu.VMEM((DA_DEPTH, 2, s, DA_BLOCK), F32),
                        pltpu.VMEM((2, s, DA_BLOCK), BF16)],
        compiler_params=_params(2),
        name="diffattn",
    )(h3, h3, h3, *lams, norm_g)


def _layer_norm(y, g, b):
    mu = jnp.mean(y, axis=-1, keepdims=True)
    d = y - mu
    var = jnp.mean(d * d, axis=-1, keepdims=True)
    return d * lax.rsqrt(var + LN_EPS) * g + b


def _merge_kernel(x_ref, oa_ref, ob_ref, oc_ref, wg_ref, bg_ref, wa_ref, wb_ref, wc_ref, wo_ref, g_ref, b_ref, o_ref):
    xf = x_ref[...]
    xb = xf.astype(BF16)
    merged = None
    for n, (br_ref, w_ref) in enumerate(((oa_ref, wa_ref), (ob_ref, wb_ref), (oc_ref, wc_ref))):
        cols = slice(n * D_MODEL, (n + 1) * D_MODEL)
        z = _dot(xb, wg_ref[:, cols]) + bg_ref[:, cols]
        term = _dot(br_ref[...], w_ref[...]) * (1.0 / (1.0 + jnp.exp(-z)))
        merged = term if merged is None else merged + term
    mix = _dot(merged.astype(BF16), wo_ref[...])
    o_ref[...] = _layer_norm(DEEPNORM_ALPHA * xf + mix, g_ref[...], b_ref[...])


def _merge(x2, oa, ob, oc, wg, bg, wa, wb, wc, wo, g, b, tm):
    t = x2.shape[0]
    row = lambda w: pl.BlockSpec((tm, w), lambda i: (i, 0))
    return pl.pallas_call(
        _merge_kernel,
        grid=(t // tm,),
        in_specs=[row(D_MODEL), row(RET_V_W), row(CA_W), row(DA_V_W),
                  _const_spec((D_MODEL, GATE_W)), _const_spec((1, GATE_W)),
                  _const_spec((RET_V_W, D_MODEL)), _const_spec((CA_W, D_MODEL)), _const_spec((DA_V_W, D_MODEL)),
                  _const_spec((D_MODEL, D_MODEL)), _const_spec((1, D_MODEL)), _const_spec((1, D_MODEL))],
        out_specs=row(D_MODEL),
        out_shape=jax.ShapeDtypeStruct((t, D_MODEL), F32),
        compiler_params=_params(1),
        name="merge",
    )(x2, oa, ob, oc, wg, bg, wa, wb, wc, wo, g, b)


FFN_CHUNK = 256


def _ffn_kernel(x_ref, wi_ref, wo_ref, g_ref, b_ref, o_ref, act_ref):
    xf = x_ref[...]
    xb = xf.astype(BF16)
    for c in range(0, FFN_HIDDEN, FFN_CHUNK):
        ug = _dot(xb, wi_ref[:, c:c + FFN_CHUNK])
        uu = _dot(xb, wi_ref[:, FFN_HIDDEN + c:FFN_HIDDEN + c + FFN_CHUNK])
        act_ref[:, c:c + FFN_CHUNK] = (ug / (1.0 + jnp.exp(-ug)) * uu).astype(BF16)
    ffn = _dot(act_ref[...], wo_ref[...])
    o_ref[...] = _layer_norm(DEEPNORM_ALPHA * xf + ffn, g_ref[...], b_ref[...])


def _ffn(x2, wi, wo, g, b, tm):
    t = x2.shape[0]
    return pl.pallas_call(
        _ffn_kernel,
        grid=(t // tm,),
        in_specs=[pl.BlockSpec((tm, D_MODEL), lambda i: (i, 0)),
                  _const_spec((D_MODEL, 2 * FFN_HIDDEN)), _const_spec((FFN_HIDDEN, D_MODEL)),
                  _const_spec((1, D_MODEL)), _const_spec((1, D_MODEL))],
        out_specs=pl.BlockSpec((tm, D_MODEL), lambda i: (i, 0)),
        out_shape=jax.ShapeDtypeStruct((t, D_MODEL), F32),
        scratch_shapes=[pltpu.VMEM((tm, FFN_HIDDEN), BF16)],
        compiler_params=_params(1),
        name="ffn",
    )(x2, wi, wo, g, b)


def _rotary_tables(seq):
    pos = jnp.arange(seq, dtype=F32)[:, None]

    def cs(d):
        inv_freq = ROPE_THETA ** (-jnp.arange(0, d, 2, dtype=F32) / d)
        ang = pos * inv_freq[None, :]
        return jnp.cos(ang), jnp.sin(ang)

    rc, rs = cs(RET_DK)
    rcos = jnp.concatenate([rc, rc], axis=1)
    rsin = jnp.concatenate([-rs, rs], axis=1)
    dc, ds = cs(DA_DH)
    z = jnp.zeros_like(ds)
    dcos = jnp.concatenate([dc, dc, dc, dc], axis=1)
    dsina = jnp.concatenate([-ds, z, -ds, z], axis=1)
    dsinb = jnp.concatenate([z, ds, z, ds], axis=1)
    return rcos, rsin, dcos, dsina, dsinb


def kernel(x, w_in, ret_norm_g, ca_rel_bias, da_lambda_q1, da_lambda_k1, da_lambda_q2, da_lambda_k2, da_norm_g, w_branch_a, w_branch_b, w_branch_c, b_merge, w_out, ln1_g, ln1_b, w_ffn_in, w_ffn_out, ln2_g, ln2_b):
    b, s, d = x.shape
    t = b * s
    tm = 512
    rot_tabs = _rotary_tables(s)
    ret_tabs = _retention_tables()
    row = lambda a: a.reshape(1, -1).astype(F32)

    x2 = x.reshape(t, d)
    for l in range(DEPTH):
        lambda_init = 0.8 - 0.6 * math.exp(-0.3 * l)
        w_mix = w_in[l, :, :MIX_W].astype(BF16)
        w_gate = w_in[l, :, MIX_W:].astype(BF16)

        h = _inproj(x2, w_mix, rot_tabs, s, tm)
        h3 = h.reshape(b, s, MIX_W)
        o_a = _retention(h3, ret_tabs, row(ret_norm_g[l]))
        o_b = _chunkattn(h3, _chunkattn_bias(ca_rel_bias[l]))
        o_c = _diffattn(h3, (row(da_lambda_q1[l]), row(da_lambda_k1[l]), row(da_lambda_q2[l]), row(da_lambda_k2[l])),
                        row(da_norm_g[l]), lambda_init)
        x2 = _merge(x2, o_a.reshape(t, RET_V_W), o_b.reshape(t, CA_W), o_c.reshape(t, DA_V_W),
                    w_gate, row(b_merge[l]), w_branch_a[l].astype(BF16), w_branch_b[l].astype(BF16),
                    w_branch_c[l].astype(BF16), w_out[l].astype(BF16), row(ln1_g[l]), row(ln1_b[l]), tm)
        x2 = _ffn(x2, w_ffn_in[l].astype(BF16), w_ffn_out[l].astype(BF16), row(ln2_g[l]), row(ln2_b[l]), tm)
    return x2.reshape(b, s, d)
```

```python
import functools
import math

import jax
import jax.numpy as jnp
import numpy as np
from jax import lax
from jax.experimental import pallas as pl
from jax.experimental.pallas import tpu as pltpu

D_MODEL = 1024
DEPTH = 4
CHUNK = 64
RET_HEADS, RET_DK, RET_DV = 4, 128, 256
CA_HEADS, CA_DH, CA_LEFT_CHUNKS, REL_CLIP = 8, 64, 8, 256
DA_HEADS, DA_DH = 4, 64
FFN_HIDDEN = -(-8 * D_MODEL // (3 * 256)) * 256
ROPE_THETA = 10000.0
LN_EPS = 1e-5
NEG_INF = -1e30
DEEPNORM_ALPHA = (2.0 * DEPTH) ** 0.25

RET_QK_W = RET_HEADS * RET_DK
RET_V_W = RET_HEADS * RET_DV
CA_W = CA_HEADS * CA_DH
DA_QK_W = 2 * DA_HEADS * DA_DH
DA_V_W = DA_HEADS * 2 * DA_DH
OFF_RQ = 0
OFF_RK = OFF_RQ + RET_QK_W
OFF_RV = OFF_RK + RET_QK_W
OFF_RG = OFF_RV + RET_V_W
OFF_CQ = OFF_RG + RET_V_W
OFF_CK = OFF_CQ + CA_W
OFF_CV = OFF_CK + CA_W
OFF_DQ = OFF_CV + CA_W
OFF_DK = OFF_DQ + DA_QK_W
OFF_DV = OFF_DK + DA_QK_W
MIX_W = OFF_DV + DA_V_W
GATE_W = 3 * D_MODEL

LOG2E = math.log2(math.e)
LANES = 128
ONES_ROWS = 16
VMEM_LIMIT = 56 * 1024 * 1024

RET_BLOCK = 256
CA_QBLOCK = 256
CA_HALF = CA_QBLOCK // 2
CA_PAD = CA_LEFT_CHUNKS * CHUNK
CA_WIN = CA_PAD + CA_QBLOCK
CA_HWIN = CA_PAD + CA_HALF
CA_AHEAD = 2
CA_DEPTH = 2 * CA_AHEAD
assert CA_HEADS % CA_DEPTH == 0
CA_SLAB = CA_DH + ONES_ROWS
DA_BLOCK = 256
DA_DEPTH = 3
SUB_ROWS = 256
PROJ_ROWS = 512
TAIL_ROWS = 4 * SUB_ROWS

BF16 = jnp.bfloat16
F32 = jnp.float32


def _dot(a, b):
    return jnp.dot(a, b, preferred_element_type=F32)


def _dot_nt(a, b):
    return lax.dot_general(a, b, (((1,), (1,)), ((), ())), preferred_element_type=F32)


def _dot_tn(a, b):
    return lax.dot_general(a, b, (((0,), (0,)), ((), ())), preferred_element_type=F32)


def _const_spec(shape):
    zeros = (0,) * len(shape)
    return pl.BlockSpec(shape, lambda *_: zeros, pipeline_mode=pl.Buffered(1))


def _params(n_axes):
    return pltpu.CompilerParams(dimension_semantics=("arbitrary",) * n_axes, vmem_limit_bytes=VMEM_LIMIT)


def _inproj_kernel(x_ref, w_ref, rcos_ref, rsin_ref, dcos_ref, dsina_ref, dsinb_ref, o_ref):
    xb = x_ref[...].astype(BF16)

    def proj(off, width):
        return _dot(xb, w_ref[:, off:off + width])

    def ret_rotary(a, scale):
        outs = []
        for h in range(RET_HEADS):
            ah = a[:, h * RET_DK:(h + 1) * RET_DK]
            r = ah * rcos_ref[...] + pltpu.roll(ah, RET_DK // 2, 1) * rsin_ref[...]
            outs.append(r * scale if scale != 1.0 else r)
        return jnp.concatenate(outs, axis=1)

    def da_rotary(a, scale):
        outs = []
        for g in range(DA_QK_W // LANES):
            ag = a[:, g * LANES:(g + 1) * LANES]
            r = (ag * dcos_ref[...] + pltpu.roll(ag, LANES - DA_DH // 2, 1) * dsina_ref[...]
                 + pltpu.roll(ag, DA_DH // 2, 1) * dsinb_ref[...])
            outs.append(r * scale if scale != 1.0 else r)
        return jnp.concatenate(outs, axis=1)

    o_ref[:, OFF_RQ:OFF_RQ + RET_QK_W] = ret_rotary(proj(OFF_RQ, RET_QK_W), 1.0).astype(BF16)
    o_ref[:, OFF_RK:OFF_RK + RET_QK_W] = ret_rotary(proj(OFF_RK, RET_QK_W), RET_DK ** -0.5).astype(BF16)
    for off in range(OFF_RV, OFF_CQ, 512):
        o_ref[:, off:off + 512] = proj(off, 512).astype(BF16)
    o_ref[:, OFF_CQ:OFF_CQ + CA_W] = (proj(OFF_CQ, CA_W) * (CA_DH ** -0.5 * LOG2E)).astype(BF16)
    o_ref[:, OFF_CK:OFF_CK + CA_W] = proj(OFF_CK, CA_W).astype(BF16)
    o_ref[:, OFF_CV:OFF_CV + CA_W] = proj(OFF_CV, CA_W).astype(BF16)
    o_ref[:, OFF_DQ:OFF_DQ + DA_QK_W] = da_rotary(proj(OFF_DQ, DA_QK_W), DA_DH ** -0.5 * LOG2E).astype(BF16)
    o_ref[:, OFF_DK:OFF_DK + DA_QK_W] = da_rotary(proj(OFF_DK, DA_QK_W), 1.0).astype(BF16)
    o_ref[:, OFF_DV:OFF_DV + DA_V_W] = proj(OFF_DV, DA_V_W).astype(BF16)


def _inproj(x2, w_mix, tabs, seq, tm):
    t = x2.shape[0]
    pos_blocks = seq // tm
    tab_spec = pl.BlockSpec((tm, LANES), lambda i: (i % pos_blocks, 0))
    return pl.pallas_call(
        _inproj_kernel,
        grid=(t // tm,),
        in_specs=[pl.BlockSpec((tm, D_MODEL), lambda i: (i, 0)),
                  _const_spec((D_MODEL, MIX_W)),
                  tab_spec, tab_spec, tab_spec, tab_spec, tab_spec],
        out_specs=pl.BlockSpec((tm, MIX_W), lambda i: (i, 0)),
        out_shape=jax.ShapeDtypeStruct((t, MIX_W), BF16),
        compiler_params=_params(1),
        name="inproj",
    )(x2, w_mix, *tabs)


def _retention_kernel(q_ref, k_ref, v_ref, g_ref, dmat_ref, qdec_ref, kdec_ref, cdec_ref, ng_ref, o_ref,
                      state_ref, sv_ref, kv_ref):
    n_blocks = q_ref.shape[0] // RET_BLOCK

    def local(c):
        rows = slice(c * RET_BLOCK, (c + 1) * RET_BLOCK)
        q = q_ref[rows, :]
        k = k_ref[rows, :]
        v = v_ref[rows, :]
        s = _dot_nt(q, k) * dmat_ref[...]
        sv_ref[c % 2] = _dot(s.astype(BF16), v)
        kd = (k.astype(F32) * kdec_ref[...]).astype(BF16)
        kv_ref[c % 2] = _dot_tn(kd, v)

    def finish(c):
        rows = slice(c * RET_BLOCK, (c + 1) * RET_BLOCK)
        o = sv_ref[c % 2]
        if c > 0:
            qd = (q_ref[rows, :].astype(F32) * qdec_ref[...]).astype(BF16)
            o = o + _dot(qd, state_ref[...].astype(BF16))
            state_ref[...] = state_ref[...] * cdec_ref[0:1, :] + kv_ref[c % 2]
        else:
            state_ref[...] = kv_ref[c % 2]
        ms = jnp.mean(o * o, axis=-1, keepdims=True)
        y = o * lax.rsqrt(ms + LN_EPS) * ng_ref[...]
        gate = g_ref[rows, :].astype(F32)
        o_ref[rows, :] = (gate / (1.0 + jnp.exp(-gate)) * y).astype(BF16)

    local(0)
    for c in range(n_blocks):
        if c + 1 < n_blocks:
            local(c + 1)
        finish(c)


def _retention(h3, tabs, norm_g):
    b, s, _ = h3.shape
    dmat, qdec, kdec, cdec = tabs
    return pl.pallas_call(
        _retention_kernel,
        grid=(b, RET_HEADS),
        in_specs=[pl.BlockSpec((None, s, RET_DK), lambda i, h: (i, 0, OFF_RQ // RET_DK + h)),
                  pl.BlockSpec((None, s, RET_DK), lambda i, h: (i, 0, OFF_RK // RET_DK + h)),
                  pl.BlockSpec((None, s, RET_DV), lambda i, h: (i, 0, OFF_RV // RET_DV + h)),
                  pl.BlockSpec((None, s, RET_DV), lambda i, h: (i, 0, OFF_RG // RET_DV + h)),
                  pl.BlockSpec((None, RET_BLOCK, RET_BLOCK), lambda i, h: (h, 0, 0)),
                  pl.BlockSpec((None, RET_BLOCK, RET_DK), lambda i, h: (h, 0, 0)),
                  pl.BlockSpec((None, RET_BLOCK, RET_DK), lambda i, h: (h, 0, 0)),
                  pl.BlockSpec((None, 8, RET_DV), lambda i, h: (h, 0, 0)),
                  pl.BlockSpec((1, RET_DV), lambda i, h: (0, 0))],
        out_specs=pl.BlockSpec((None, s, RET_DV), lambda i, h: (i, 0, h)),
        out_shape=jax.ShapeDtypeStruct((b, s, RET_V_W), BF16),
        scratch_shapes=[pltpu.VMEM((RET_DK, RET_DV), F32),
                        pltpu.VMEM((2, RET_BLOCK, RET_DV), F32),
                        pltpu.VMEM((2, RET_DK, RET_DV), F32)],
        compiler_params=_params(2),
        name="retention",
    )(h3, h3, h3, h3, dmat, qdec, kdec, cdec, norm_g)


def _retention_tables():
    h = np.arange(RET_HEADS, dtype=np.float64)
    log_gamma = np.log1p(-np.exp2(-5.0 - h))
    n = np.arange(RET_BLOCK)
    diff = (n[:, None] - n[None, :]).astype(np.float64)
    cn, cm = n[:, None] // CHUNK, n[None, :] // CHUNK
    expo = np.where(cm == cn, np.abs(diff), diff)
    dmat = np.where(cm <= cn, np.exp(log_gamma[:, None, None] * expo[None]), 0.0)
    qdec = np.exp(log_gamma[:, None] * (n[None, :] + 1.0))
    kdec = np.exp(log_gamma[:, None] * (RET_BLOCK - 1.0 - n[None, :]))
    cdec = np.exp(log_gamma * RET_BLOCK)
    qdec = np.broadcast_to(qdec[:, :, None], (RET_HEADS, RET_BLOCK, RET_DK))
    kdec = np.broadcast_to(kdec[:, :, None], (RET_HEADS, RET_BLOCK, RET_DK))
    cdec = np.broadcast_to(cdec[:, None, None], (RET_HEADS, 8, RET_DV))
    return tuple(jnp.asarray(a, F32) for a in (dmat, qdec, kdec, cdec))


def _chunkattn_kernel(q_ref, k_ref, v_ref, bias_ref, o_ref, kpad_ref, vt_ref, s_ref, p_ref, ot_ref):
    s_len = q_ref.shape[0]
    n_blocks = s_len // CA_QBLOCK
    n_pad = CA_PAD // CA_QBLOCK
    kpad_ref[0:CA_PAD, :] = jnp.zeros((CA_PAD, CA_W), BF16)
    kpad_ref[CA_PAD:, :] = k_ref[...]
    ones = jnp.ones((ONES_ROWS, CA_QBLOCK), BF16)
    for c in range(n_pad + n_blocks):
        if c < n_pad:
            vt = jnp.zeros((CA_W, CA_QBLOCK), BF16)
        else:
            vt = v_ref[(c - n_pad) * CA_QBLOCK:(c - n_pad + 1) * CA_QBLOCK, :].astype(F32).T.astype(BF16)
        for h in range(CA_HEADS):
            vt_ref[c, h * CA_SLAB:h * CA_SLAB + CA_DH, :] = vt[h * CA_DH:(h + 1) * CA_DH, :]
            vt_ref[c, h * CA_SLAB + CA_DH:(h + 1) * CA_SLAB, :] = ones
    p_ref[:, CA_HWIN:CA_WIN, 0:CA_HALF] = jnp.zeros((2, CA_WIN - CA_HWIN, CA_HALF), BF16)
    p_ref[:, 0:CA_HALF, CA_HALF:CA_QBLOCK] = jnp.zeros((2, CA_HALF, CA_HALF), BF16)
    lane = lax.broadcasted_iota(jnp.int32, (CA_QBLOCK, LANES), 1)
    head_lanes = (lane < CA_DH, lane >= CA_DH)
    win_row = lax.broadcasted_iota(jnp.int32, (CA_HWIN, CA_HALF), 0)

    def row0(i):
        return i * CA_QBLOCK if isinstance(i, int) else pl.multiple_of(i * CA_QBLOCK, CA_QBLOCK)

    def scores(i, h):
        pair = slice((h // 2) * LANES, (h // 2 + 1) * LANES)
        q = q_ref[pl.ds(row0(i), CA_QBLOCK), pair]
        qz = jnp.where(head_lanes[h % 2], q, jnp.zeros_like(q))
        s_ref[h % CA_DEPTH] = _dot_nt(kpad_ref[pl.ds(row0(i), CA_WIN), pair], qz)

    def attend(i, h, first_key):
        for half in range(2):
            rows = slice(half * CA_HALF, half * CA_HALF + CA_HWIN)
            cols = slice(half * CA_HALF, (half + 1) * CA_HALF)
            s = s_ref[h % CA_DEPTH, rows, cols] + bias_ref[h]
            if first_key is not None:
                s = jnp.where(win_row >= -(first_key + half * CA_HALF), s, NEG_INF)
            m = jnp.max(s, axis=0, keepdims=True)
            p_ref[h % 2, rows, cols] = jnp.exp2(s - m).astype(BF16)
        pv = None
        for c in range(CA_WIN // CA_QBLOCK):
            part = _dot(vt_ref[i + c, h * CA_SLAB:(h + 1) * CA_SLAB, :],
                        p_ref[h % 2, c * CA_QBLOCK:(c + 1) * CA_QBLOCK, :])
            pv = part if pv is None else pv + part
        ot_ref[h * CA_DH:(h + 1) * CA_DH, :] = pv[0:CA_DH] * (1.0 / pv[CA_DH:CA_DH + 1])

    def qblock(i, first_key, nxt):
        for h in range(CA_HEADS):
            ahead = h + CA_AHEAD
            if ahead < CA_HEADS:
                scores(i, ahead)
            elif nxt is not None:
                scores(nxt, ahead - CA_HEADS)
            attend(i, h, first_key)
        o_ref[pl.ds(row0(i), CA_QBLOCK), :] = ot_ref[...].T.astype(BF16)

    for h in range(CA_AHEAD):
        scores(0, h)
    for i in range(n_pad):
        qblock(i, i * CA_QBLOCK - CA_PAD, i + 1)

    def body(i, carry):
        qblock(i, None, i + 1)
        return carry

    lax.fori_loop(n_pad, n_blocks - 1, body, 0)
    qblock(n_blocks - 1, None, None)


def _chunkattn(h3, bias):
    b, s, _ = h3.shape
    n_blocks = s // CA_QBLOCK
    return pl.pallas_call(
        _chunkattn_kernel,
        grid=(b,),
        in_specs=[pl.BlockSpec((None, s, CA_W), lambda i: (i, 0, OFF_CQ // CA_W)),
                  pl.BlockSpec((None, s, CA_W), lambda i: (i, 0, OFF_CK // CA_W)),
                  pl.BlockSpec((None, s, CA_W), lambda i: (i, 0, OFF_CV // CA_W)),
                  _const_spec((CA_HEADS, CA_HWIN, CA_HALF))],
        out_specs=pl.BlockSpec((None, s, CA_W), lambda i: (i, 0, 0)),
        out_shape=jax.ShapeDtypeStruct((b, s, CA_W), BF16),
        scratch_shapes=[pltpu.VMEM((s + CA_PAD, CA_W), BF16),
                        pltpu.VMEM((n_blocks + CA_PAD // CA_QBLOCK, CA_HEADS * CA_SLAB, CA_QBLOCK), BF16),
                        pltpu.VMEM((CA_DEPTH, CA_WIN, CA_QBLOCK), F32),
                        pltpu.VMEM((2, CA_WIN, CA_QBLOCK), BF16),
                        pltpu.VMEM((CA_W, CA_QBLOCK), F32)],
        compiler_params=_params(1),
        name="chunkattn",
    )(h3, h3, h3, bias)


def _chunkattn_bias(rel_bias):
    j = np.arange(CA_HWIN)[:, None]
    r = np.arange(CA_HALF)[None, :]
    qc, kc = r // CHUNK, j // CHUNK
    band = (kc >= qc) & (kc <= qc + CA_LEFT_CHUNKS)
    n_diag = CA_HWIN + CA_HALF
    u = np.arange(n_diag)
    idx = np.clip(u - (CA_HWIN - 1) + CA_PAD, -REL_CLIP, REL_CLIP) + REL_CLIP
    diag = rel_bias[:, idx].astype(F32)
    shifted = jnp.tile(diag, (1, CA_HWIN + 1))[:, :CA_HWIN * (n_diag + 1)].reshape(CA_HEADS, CA_HWIN, n_diag + 1)
    table = shifted[:, ::-1, :CA_HALF]
    return jnp.where(jnp.asarray(band)[None], table * LOG2E, NEG_INF)


def _diffattn_kernel(lambda_init, q_ref, k_ref, v_ref, lq1_ref, lk1_ref, lq2_ref, lk2_ref, ng_ref, o_ref,
                     vt_ref, s_ref, p_ref):
    s_len = q_ref.shape[0]
    n_blocks = s_len // DA_BLOCK
    hw = 2 * DA_DH
    lam = (jnp.exp(jnp.sum(lq1_ref[...] * lk1_ref[...], axis=-1, keepdims=True))
           - jnp.exp(jnp.sum(lq2_ref[...] * lk2_ref[...], axis=-1, keepdims=True)) + lambda_init)
    for c in range(n_blocks):
        blk = slice(c * DA_BLOCK, (c + 1) * DA_BLOCK)
        vt_ref[0:hw, blk] = v_ref[blk, :].astype(F32).T.astype(BF16)
    vt_ref[hw:, :] = jnp.ones((ONES_ROWS, s_len), BF16)
    key_chunk = lax.broadcasted_iota(jnp.int32, (DA_BLOCK, DA_BLOCK), 0) // CHUNK
    query_chunk = lax.broadcasted_iota(jnp.int32, (DA_BLOCK, DA_BLOCK), 1) // CHUNK
    diag_allowed = key_chunk <= query_chunk
    lane = lax.broadcasted_iota(jnp.int32, (DA_BLOCK, hw), 1)
    head_lanes = (lane < DA_DH, lane >= DA_DH)

    def scores(i, buf):
        q = q_ref[i * DA_BLOCK:(i + 1) * DA_BLOCK, :]
        maxima = []
        for t in range(2):
            qz = jnp.where(head_lanes[t], q, jnp.zeros_like(q))
            m = None
            for c in range(i + 1):
                blk = slice(c * DA_BLOCK, (c + 1) * DA_BLOCK)
                s = _dot_nt(k_ref[blk, :], qz)
                if c == i:
                    s = jnp.where(diag_allowed, s, NEG_INF)
                s_ref[buf, t, blk, :] = s
                mc = jnp.max(s, axis=0, keepdims=True)
                m = mc if m is None else jnp.maximum(m, mc)
            maxima.append(m)
        return maxima

    def attend(i, buf, maxima):
        n = (i + 1) * DA_BLOCK
        heads = []
        for t in range(2):
            for c in range(i + 1):
                blk = slice(c * DA_BLOCK, (c + 1) * DA_BLOCK)
                p_ref[t, blk, :] = jnp.exp2(s_ref[buf, t, blk, :] - maxima[t]).astype(BF16)
            pv = _dot(vt_ref[:, 0:n], p_ref[t, 0:n, :])
            heads.append(pv[0:hw] * (1.0 / pv[hw:hw + 1]))
        o_t = heads[0] - lam * heads[1]
        ms = jnp.mean(o_t * o_t, axis=0, keepdims=True)
        y = (o_t * lax.rsqrt(ms + LN_EPS)).T * ng_ref[...] * (1.0 - lambda_init)
        o_ref[i * DA_BLOCK:(i + 1) * DA_BLOCK, :] = y.astype(BF16)

    order = [0] + list(range(n_blocks - 1, 0, -1))
    maxima = {}
    for pos in range(n_blocks + DA_DEPTH - 1):
        if pos < n_blocks:
            maxima[pos] = scores(order[pos], pos % DA_DEPTH)
        done = pos - (DA_DEPTH - 1)
        if done >= 0:
            attend(order[done], done % DA_DEPTH, maxima.pop(done))


def _diffattn(h3, lams, norm_g, lambda_init):
    b, s, _ = h3.shape
    hw = 2 * DA_DH
    small = pl.BlockSpec((1, DA_DH), lambda i, h: (0, 0))
    return pl.pallas_call(
        functools.partial(_diffattn_kernel, lambda_init),
        grid=(b, DA_HEADS),
        in_specs=[pl.BlockSpec((None, s, hw), lambda i, h: (i, 0, OFF_DQ // hw + h)),
                  pl.BlockSpec((None, s, hw), lambda i, h: (i, 0, OFF_DK // hw + h)),
                  pl.BlockSpec((None, s, hw), lambda i, h: (i, 0, OFF_DV // hw + h)),
                  small, small, small, small,
                  pl.BlockSpec((1, hw), lambda i, h: (0, 0))],
        out_specs=pl.BlockSpec((None, s, hw), lambda i, h: (i, 0, h)),
        out_shape=jax.ShapeDtypeStruct((b, s, DA_V_W), BF16),
        scratch_shapes=[pltpu.VMEM((hw + ONES_ROWS, s), BF16),
                        pltpu.VMEM((DA_DEPTH, 2, s, DA_BLOCK), F32),
                        pltpu.VMEM((2, s, DA_BLOCK), BF16)],
        compiler_params=_params(2),
        name="diffattn",
    )(h3, h3, h3, *lams, norm_g)


def _layer_norm(y, g, b):
    mu = jnp.mean(y, axis=-1, keepdims=True)
    d = y - mu
    var = jnp.mean(d * d, axis=-1, keepdims=True)
    return d * lax.rsqrt(var + LN_EPS) * g + b


def _merge_kernel(x_ref, oa_ref, ob_ref, oc_ref, wg_ref, bg_ref, wa_ref, wb_ref, wc_ref, wo_ref, g_ref, b_ref, o_ref):
    for r0 in range(0, x_ref.shape[0], SUB_ROWS):
        rows = slice(r0, r0 + SUB_ROWS)
        xf = x_ref[rows, :]
        xb = xf.astype(BF16)
        merged = None
        for n, (br_ref, w_ref) in enumerate(((oa_ref, wa_ref), (ob_ref, wb_ref), (oc_ref, wc_ref))):
            cols = slice(n * D_MODEL, (n + 1) * D_MODEL)
            z = _dot(xb, wg_ref[:, cols]) + bg_ref[:, cols]
            term = _dot(br_ref[rows, :], w_ref[...]) * (1.0 / (1.0 + jnp.exp(-z)))
            merged = term if merged is None else merged + term
        mix = _dot(merged.astype(BF16), wo_ref[...])
        o_ref[rows, :] = _layer_norm(DEEPNORM_ALPHA * xf + mix, g_ref[...], b_ref[...])


def _merge(x2, oa, ob, oc, wg, bg, wa, wb, wc, wo, g, b, tm):
    t = x2.shape[0]
    row = lambda w: pl.BlockSpec((tm, w), lambda i: (i, 0))
    return pl.pallas_call(
        _merge_kernel,
        grid=(t // tm,),
        in_specs=[row(D_MODEL), row(RET_V_W), row(CA_W), row(DA_V_W),
                  _const_spec((D_MODEL, GATE_W)), _const_spec((1, GATE_W)),
                  _const_spec((RET_V_W, D_MODEL)), _const_spec((CA_W, D_MODEL)), _const_spec((DA_V_W, D_MODEL)),
                  _const_spec((D_MODEL, D_MODEL)), _const_spec((1, D_MODEL)), _const_spec((1, D_MODEL))],
        out_specs=row(D_MODEL),
        out_shape=jax.ShapeDtypeStruct((t, D_MODEL), F32),
        compiler_params=_params(1),
        name="merge",
    )(x2, oa, ob, oc, wg, bg, wa, wb, wc, wo, g, b)


FFN_CHUNK = 256


def _ffn_kernel(x_ref, wi_ref, wo_ref, g_ref, b_ref, o_ref, act_ref):
    for r0 in range(0, x_ref.shape[0], SUB_ROWS):
        rows = slice(r0, r0 + SUB_ROWS)
        xf = x_ref[rows, :]
        xb = xf.astype(BF16)
        for c in range(0, FFN_HIDDEN, FFN_CHUNK):
            ug = _dot(xb, wi_ref[:, c:c + FFN_CHUNK])
            uu = _dot(xb, wi_ref[:, FFN_HIDDEN + c:FFN_HIDDEN + c + FFN_CHUNK])
            act_ref[rows, c:c + FFN_CHUNK] = (ug / (1.0 + jnp.exp(-ug)) * uu).astype(BF16)
        ffn = _dot(act_ref[rows, :], wo_ref[...])
        o_ref[rows, :] = _layer_norm(DEEPNORM_ALPHA * xf + ffn, g_ref[...], b_ref[...])


def _ffn(x2, wi, wo, g, b, tm):
    t = x2.shape[0]
    return pl.pallas_call(
        _ffn_kernel,
        grid=(t // tm,),
        in_specs=[pl.BlockSpec((tm, D_MODEL), lambda i: (i, 0)),
                  _const_spec((D_MODEL, 2 * FFN_HIDDEN)), _const_spec((FFN_HIDDEN, D_MODEL)),
                  _const_spec((1, D_MODEL)), _const_spec((1, D_MODEL))],
        out_specs=pl.BlockSpec((tm, D_MODEL), lambda i: (i, 0)),
        out_shape=jax.ShapeDtypeStruct((t, D_MODEL), F32),
        scratch_shapes=[pltpu.VMEM((tm, FFN_HIDDEN), BF16)],
        compiler_params=_params(1),
        name="ffn",
    )(x2, wi, wo, g, b)


def _rotary_tables(seq):
    pos = jnp.arange(seq, dtype=F32)[:, None]

    def cs(d):
        inv_freq = ROPE_THETA ** (-jnp.arange(0, d, 2, dtype=F32) / d)
        ang = pos * inv_freq[None, :]
        return jnp.cos(ang), jnp.sin(ang)

    rc, rs = cs(RET_DK)
    rcos = jnp.concatenate([rc, rc], axis=1)
    rsin = jnp.concatenate([-rs, rs], axis=1)
    dc, ds = cs(DA_DH)
    z = jnp.zeros_like(ds)
    dcos = jnp.concatenate([dc, dc, dc, dc], axis=1)
    dsina = jnp.concatenate([-ds, z, -ds, z], axis=1)
    dsinb = jnp.concatenate([z, ds, z, ds], axis=1)
    return rcos, rsin, dcos, dsina, dsinb


def kernel(x, w_in, ret_norm_g, ca_rel_bias, da_lambda_q1, da_lambda_k1, da_lambda_q2, da_lambda_k2, da_norm_g, w_branch_a, w_branch_b, w_branch_c, b_merge, w_out, ln1_g, ln1_b, w_ffn_in, w_ffn_out, ln2_g, ln2_b):
    b, s, d = x.shape
    t = b * s
    rot_tabs = _rotary_tables(s)
    ret_tabs = _retention_tables()
    row = lambda a: a.reshape(1, -1).astype(F32)

    x2 = x.reshape(t, d)
    for l in range(DEPTH):
        lambda_init = 0.8 - 0.6 * math.exp(-0.3 * l)
        w_mix = w_in[l, :, :MIX_W].astype(BF16)
        w_gate = w_in[l, :, MIX_W:].astype(BF16)

        h = _inproj(x2, w_mix, rot_tabs, s, PROJ_ROWS)
        h3 = h.reshape(b, s, MIX_W)
        o_a = _retention(h3, ret_tabs, row(ret_norm_g[l]))
        o_b = _chunkattn(h3, _chunkattn_bias(ca_rel_bias[l]))
        o_c = _diffattn(h3, (row(da_lambda_q1[l]), row(da_lambda_k1[l]), row(da_lambda_q2[l]), row(da_lambda_k2[l])),
                        row(da_norm_g[l]), lambda_init)
        x2 = _merge(x2, o_a.reshape(t, RET_V_W), o_b.reshape(t, CA_W), o_c.reshape(t, DA_V_W),
                    w_gate, row(b_merge[l]), w_branch_a[l].astype(BF16), w_branch_b[l].astype(BF16),
                    w_branch_c[l].astype(BF16), w_out[l].astype(BF16), row(ln1_g[l]), row(ln1_b[l]), TAIL_ROWS)
        x2 = _ffn(x2, w_ffn_in[l].astype(BF16), w_ffn_out[l].astype(BF16), row(ln2_g[l]), row(ln2_b[l]), TAIL_ROWS)
    return x2.reshape(b, s, d)
```

```python
import functools
import math

import jax
import jax.numpy as jnp
import numpy as np
from jax import lax
from jax.experimental import pallas as pl
from jax.experimental.pallas import tpu as pltpu

D_MODEL = 1024
DEPTH = 4
CHUNK = 64
RET_HEADS, RET_DK, RET_DV = 4, 128, 256
CA_HEADS, CA_DH, CA_LEFT_CHUNKS, REL_CLIP = 8, 64, 8, 256
DA_HEADS, DA_DH = 4, 64
FFN_HIDDEN = -(-8 * D_MODEL // (3 * 256)) * 256
ROPE_THETA = 10000.0
LN_EPS = 1e-5
NEG_INF = -1e30
DEEPNORM_ALPHA = (2.0 * DEPTH) ** 0.25

RET_QK_W = RET_HEADS * RET_DK
RET_V_W = RET_HEADS * RET_DV
CA_W = CA_HEADS * CA_DH
DA_QK_W = 2 * DA_HEADS * DA_DH
DA_V_W = DA_HEADS * 2 * DA_DH
OFF_RQ = 0
OFF_RK = OFF_RQ + RET_QK_W
OFF_RV = OFF_RK + RET_QK_W
OFF_RG = OFF_RV + RET_V_W
OFF_CQ = OFF_RG + RET_V_W
OFF_CK = OFF_CQ + CA_W
OFF_CV = OFF_CK + CA_W
OFF_DQ = OFF_CV + CA_W
OFF_DK = OFF_DQ + DA_QK_W
OFF_DV = OFF_DK + DA_QK_W
MIX_W = OFF_DV + DA_V_W
GATE_W = 3 * D_MODEL

LOG2E = math.log2(math.e)
LANES = 128
ONES_ROWS = 16
VMEM_LIMIT = 56 * 1024 * 1024

RET_BLOCK = 256
CA_QBLOCK = 256
CA_HALF = CA_QBLOCK // 2
CA_PAD = CA_LEFT_CHUNKS * CHUNK
CA_WIN = CA_PAD + CA_QBLOCK
CA_HWIN = CA_PAD + CA_HALF
CA_AHEAD = 2
CA_DEPTH = 2 * CA_AHEAD
assert CA_HEADS % CA_DEPTH == 0
CA_SLAB = CA_DH + ONES_ROWS
DA_BLOCK = 256
DA_AHEAD = 6
DA_RING = DA_AHEAD + 1
SUB_ROWS = 256
PROJ_ROWS = 512
TAIL_ROWS = 4 * SUB_ROWS

BF16 = jnp.bfloat16
F32 = jnp.float32


def _dot(a, b):
    return jnp.dot(a, b, preferred_element_type=F32)


def _dot_nt(a, b):
    return lax.dot_general(a, b, (((1,), (1,)), ((), ())), preferred_element_type=F32)


def _dot_tn(a, b):
    return lax.dot_general(a, b, (((0,), (0,)), ((), ())), preferred_element_type=F32)


def _const_spec(shape):
    zeros = (0,) * len(shape)
    return pl.BlockSpec(shape, lambda *_: zeros, pipeline_mode=pl.Buffered(1))


def _params(n_axes):
    return pltpu.CompilerParams(dimension_semantics=("arbitrary",) * n_axes, vmem_limit_bytes=VMEM_LIMIT)


def _inproj_kernel(layer_ref, x_ref, w_ref, rcos_ref, rsin_ref, dcos_ref, dsina_ref, dsinb_ref, o_ref):
    xb = x_ref[...].astype(BF16)

    def proj(off, width):
        return _dot(xb, w_ref[:, off:off + width])

    def ret_rotary(a, scale):
        outs = []
        for h in range(RET_HEADS):
            ah = a[:, h * RET_DK:(h + 1) * RET_DK]
            r = ah * rcos_ref[...] + pltpu.roll(ah, RET_DK // 2, 1) * rsin_ref[...]
            outs.append(r * scale if scale != 1.0 else r)
        return jnp.concatenate(outs, axis=1)

    def da_rotary(a, scale):
        outs = []
        for g in range(DA_QK_W // LANES):
            ag = a[:, g * LANES:(g + 1) * LANES]
            r = (ag * dcos_ref[...] + pltpu.roll(ag, LANES - DA_DH // 2, 1) * dsina_ref[...]
                 + pltpu.roll(ag, DA_DH // 2, 1) * dsinb_ref[...])
            outs.append(r * scale if scale != 1.0 else r)
        return jnp.concatenate(outs, axis=1)

    o_ref[:, OFF_RQ:OFF_RQ + RET_QK_W] = ret_rotary(proj(OFF_RQ, RET_QK_W), 1.0).astype(BF16)
    o_ref[:, OFF_RK:OFF_RK + RET_QK_W] = ret_rotary(proj(OFF_RK, RET_QK_W), RET_DK ** -0.5).astype(BF16)
    for off in range(OFF_RV, OFF_CQ, 512):
        o_ref[:, off:off + 512] = proj(off, 512).astype(BF16)
    o_ref[:, OFF_CQ:OFF_CQ + CA_W] = (proj(OFF_CQ, CA_W) * (CA_DH ** -0.5 * LOG2E)).astype(BF16)
    o_ref[:, OFF_CK:OFF_CK + CA_W] = proj(OFF_CK, CA_W).astype(BF16)
    o_ref[:, OFF_CV:OFF_CV + CA_W] = proj(OFF_CV, CA_W).astype(BF16)
    o_ref[:, OFF_DQ:OFF_DQ + DA_QK_W] = da_rotary(proj(OFF_DQ, DA_QK_W), DA_DH ** -0.5 * LOG2E).astype(BF16)
    o_ref[:, OFF_DK:OFF_DK + DA_QK_W] = da_rotary(proj(OFF_DK, DA_QK_W), 1.0).astype(BF16)
    o_ref[:, OFF_DV:OFF_DV + DA_V_W] = proj(OFF_DV, DA_V_W).astype(BF16)


def _inproj(layer, x2, w_in, tabs, seq, tm):
    t = x2.shape[0]
    pos_blocks = seq // tm
    tab_spec = pl.BlockSpec((tm, LANES), lambda i, l: (i % pos_blocks, 0))
    return pl.pallas_call(
        _inproj_kernel,
        grid_spec=pltpu.PrefetchScalarGridSpec(
            num_scalar_prefetch=1,
            grid=(t // tm,),
            in_specs=[pl.BlockSpec((tm, D_MODEL), lambda i, l: (i, 0)),
                      pl.BlockSpec((None, D_MODEL, MIX_W), lambda i, l: (l[0], 0, 0), pipeline_mode=pl.Buffered(1)),
                      tab_spec, tab_spec, tab_spec, tab_spec, tab_spec],
            out_specs=pl.BlockSpec((tm, MIX_W), lambda i, l: (i, 0))),
        out_shape=jax.ShapeDtypeStruct((t, MIX_W), BF16),
        compiler_params=_params(1),
        name="inproj",
    )(layer, x2, w_in, *tabs)


def _retention_kernel(q_ref, k_ref, v_ref, g_ref, dmat_ref, qdec_ref, kdec_ref, cdec_ref, ng_ref, o_ref,
                      state_ref, sv_ref, kv_ref):
    n_blocks = q_ref.shape[0] // RET_BLOCK

    def local(c):
        rows = slice(c * RET_BLOCK, (c + 1) * RET_BLOCK)
        q = q_ref[rows, :]
        k = k_ref[rows, :]
        v = v_ref[rows, :]
        s = _dot_nt(q, k) * dmat_ref[...]
        sv_ref[c % 2] = _dot(s.astype(BF16), v)
        kd = (k.astype(F32) * kdec_ref[...]).astype(BF16)
        kv_ref[c % 2] = _dot_tn(kd, v)

    def finish(c):
        rows = slice(c * RET_BLOCK, (c + 1) * RET_BLOCK)
        o = sv_ref[c % 2]
        if c > 0:
            qd = (q_ref[rows, :].astype(F32) * qdec_ref[...]).astype(BF16)
            o = o + _dot(qd, state_ref[...].astype(BF16))
            state_ref[...] = state_ref[...] * cdec_ref[0:1, :] + kv_ref[c % 2]
        else:
            state_ref[...] = kv_ref[c % 2]
        ms = jnp.mean(o * o, axis=-1, keepdims=True)
        y = o * lax.rsqrt(ms + LN_EPS) * ng_ref[...]
        gate = g_ref[rows, :].astype(F32)
        o_ref[rows, :] = (gate / (1.0 + jnp.exp(-gate)) * y).astype(BF16)

    local(0)
    for c in range(n_blocks):
        if c + 1 < n_blocks:
            local(c + 1)
        finish(c)


def _retention(h3, tabs, norm_g):
    b, s, _ = h3.shape
    dmat, qdec, kdec, cdec = tabs
    return pl.pallas_call(
        _retention_kernel,
        grid=(b, RET_HEADS),
        in_specs=[pl.BlockSpec((None, s, RET_DK), lambda i, h: (i, 0, OFF_RQ // RET_DK + h)),
                  pl.BlockSpec((None, s, RET_DK), lambda i, h: (i, 0, OFF_RK // RET_DK + h)),
                  pl.BlockSpec((None, s, RET_DV), lambda i, h: (i, 0, OFF_RV // RET_DV + h)),
                  pl.BlockSpec((None, s, RET_DV), lambda i, h: (i, 0, OFF_RG // RET_DV + h)),
                  pl.BlockSpec((None, RET_BLOCK, RET_BLOCK), lambda i, h: (h, 0, 0)),
                  pl.BlockSpec((None, RET_BLOCK, RET_DK), lambda i, h: (h, 0, 0)),
                  pl.BlockSpec((None, RET_BLOCK, RET_DK), lambda i, h: (h, 0, 0)),
                  pl.BlockSpec((None, 8, RET_DV), lambda i, h: (h, 0, 0)),
                  pl.BlockSpec((1, RET_DV), lambda i, h: (0, 0))],
        out_specs=pl.BlockSpec((None, s, RET_DV), lambda i, h: (i, 0, h)),
        out_shape=jax.ShapeDtypeStruct((b, s, RET_V_W), BF16),
        scratch_shapes=[pltpu.VMEM((RET_DK, RET_DV), F32),
                        pltpu.VMEM((2, RET_BLOCK, RET_DV), F32),
                        pltpu.VMEM((2, RET_DK, RET_DV), F32)],
        compiler_params=_params(2),
        name="retention",
    )(h3, h3, h3, h3, dmat, qdec, kdec, cdec, norm_g)


def _retention_tables():
    h = np.arange(RET_HEADS, dtype=np.float64)
    log_gamma = np.log1p(-np.exp2(-5.0 - h))
    n = np.arange(RET_BLOCK)
    diff = (n[:, None] - n[None, :]).astype(np.float64)
    cn, cm = n[:, None] // CHUNK, n[None, :] // CHUNK
    expo = np.where(cm == cn, np.abs(diff), diff)
    dmat = np.where(cm <= cn, np.exp(log_gamma[:, None, None] * expo[None]), 0.0)
    qdec = np.exp(log_gamma[:, None] * (n[None, :] + 1.0))
    kdec = np.exp(log_gamma[:, None] * (RET_BLOCK - 1.0 - n[None, :]))
    cdec = np.exp(log_gamma * RET_BLOCK)
    qdec = np.broadcast_to(qdec[:, :, None], (RET_HEADS, RET_BLOCK, RET_DK))
    kdec = np.broadcast_to(kdec[:, :, None], (RET_HEADS, RET_BLOCK, RET_DK))
    cdec = np.broadcast_to(cdec[:, None, None], (RET_HEADS, 8, RET_DV))
    return tuple(jnp.asarray(a, F32) for a in (dmat, qdec, kdec, cdec))


def _chunkattn_kernel(q_ref, k_ref, v_ref, bias_ref, o_ref, kpad_ref, vt_ref, s_ref, p_ref, ot_ref):
    s_len = q_ref.shape[0]
    n_blocks = s_len // CA_QBLOCK
    n_pad = CA_PAD // CA_QBLOCK
    kpad_ref[0:CA_PAD, :] = jnp.zeros((CA_PAD, CA_W), BF16)
    kpad_ref[CA_PAD:, :] = k_ref[...]
    ones = jnp.ones((ONES_ROWS, CA_QBLOCK), BF16)
    for c in range(n_pad + n_blocks):
        if c < n_pad:
            vt = jnp.zeros((CA_W, CA_QBLOCK), BF16)
        else:
            vt = v_ref[(c - n_pad) * CA_QBLOCK:(c - n_pad + 1) * CA_QBLOCK, :].astype(F32).T.astype(BF16)
        for h in range(CA_HEADS):
            vt_ref[c, h * CA_SLAB:h * CA_SLAB + CA_DH, :] = vt[h * CA_DH:(h + 1) * CA_DH, :]
            vt_ref[c, h * CA_SLAB + CA_DH:(h + 1) * CA_SLAB, :] = ones
    p_ref[:, CA_HWIN:CA_WIN, 0:CA_HALF] = jnp.zeros((2, CA_WIN - CA_HWIN, CA_HALF), BF16)
    p_ref[:, 0:CA_HALF, CA_HALF:CA_QBLOCK] = jnp.zeros((2, CA_HALF, CA_HALF), BF16)
    lane = lax.broadcasted_iota(jnp.int32, (CA_QBLOCK, LANES), 1)
    head_lanes = (lane < CA_DH, lane >= CA_DH)
    win_row = lax.broadcasted_iota(jnp.int32, (CA_HWIN, CA_HALF), 0)

    def row0(i):
        return i * CA_QBLOCK if isinstance(i, int) else pl.multiple_of(i * CA_QBLOCK, CA_QBLOCK)

    def scores(i, h):
        pair = slice((h // 2) * LANES, (h // 2 + 1) * LANES)
        q = q_ref[pl.ds(row0(i), CA_QBLOCK), pair]
        qz = jnp.where(head_lanes[h % 2], q, jnp.zeros_like(q))
        s_ref[h % CA_DEPTH] = _dot_nt(kpad_ref[pl.ds(row0(i), CA_WIN), pair], qz)

    def attend(i, h, first_key):
        for half in range(2):
            rows = slice(half * CA_HALF, half * CA_HALF + CA_HWIN)
            cols = slice(half * CA_HALF, (half + 1) * CA_HALF)
            s = s_ref[h % CA_DEPTH, rows, cols] + bias_ref[h]
            if first_key is not None:
                s = jnp.where(win_row >= -(first_key + half * CA_HALF), s, NEG_INF)
            m = jnp.max(s, axis=0, keepdims=True)
            p_ref[h % 2, rows, cols] = jnp.exp2(s - m).astype(BF16)
        pv = None
        for c in range(CA_WIN // CA_QBLOCK):
            part = _dot(vt_ref[i + c, h * CA_SLAB:(h + 1) * CA_SLAB, :],
                        p_ref[h % 2, c * CA_QBLOCK:(c + 1) * CA_QBLOCK, :])
            pv = part if pv is None else pv + part
        ot_ref[h * CA_DH:(h + 1) * CA_DH, :] = pv[0:CA_DH] * (1.0 / pv[CA_DH:CA_DH + 1])

    def qblock(i, first_key, nxt):
        for h in range(CA_HEADS):
            ahead = h + CA_AHEAD
            if ahead < CA_HEADS:
                scores(i, ahead)
            elif nxt is not None:
                scores(nxt, ahead - CA_HEADS)
            attend(i, h, first_key)
        o_ref[pl.ds(row0(i), CA_QBLOCK), :] = ot_ref[...].T.astype(BF16)

    for h in range(CA_AHEAD):
        scores(0, h)
    for i in range(n_pad):
        qblock(i, i * CA_QBLOCK - CA_PAD, i + 1)

    def body(i, carry):
        qblock(i, None, i + 1)
        return carry

    lax.fori_loop(n_pad, n_blocks - 1, body, 0)
    qblock(n_blocks - 1, None, None)


def _chunkattn(h3, bias):
    b, s, _ = h3.shape
    n_blocks = s // CA_QBLOCK
    return pl.pallas_call(
        _chunkattn_kernel,
        grid=(b,),
        in_specs=[pl.BlockSpec((None, s, CA_W), lambda i: (i, 0, OFF_CQ // CA_W)),
                  pl.BlockSpec((None, s, CA_W), lambda i: (i, 0, OFF_CK // CA_W)),
                  pl.BlockSpec((None, s, CA_W), lambda i: (i, 0, OFF_CV // CA_W)),
                  _const_spec((CA_HEADS, CA_HWIN, CA_HALF))],
        out_specs=pl.BlockSpec((None, s, CA_W), lambda i: (i, 0, 0)),
        out_shape=jax.ShapeDtypeStruct((b, s, CA_W), BF16),
        scratch_shapes=[pltpu.VMEM((s + CA_PAD, CA_W), BF16),
                        pltpu.VMEM((n_blocks + CA_PAD // CA_QBLOCK, CA_HEADS * CA_SLAB, CA_QBLOCK), BF16),
                        pltpu.VMEM((CA_DEPTH, CA_WIN, CA_QBLOCK), F32),
                        pltpu.VMEM((2, CA_WIN, CA_QBLOCK), BF16),
                        pltpu.VMEM((CA_W, CA_QBLOCK), F32)],
        compiler_params=_params(1),
        name="chunkattn",
    )(h3, h3, h3, bias)


def _chunkattn_bias(rel_bias):
    j = np.arange(CA_HWIN)[:, None]
    r = np.arange(CA_HALF)[None, :]
    qc, kc = r // CHUNK, j // CHUNK
    band = (kc >= qc) & (kc <= qc + CA_LEFT_CHUNKS)
    n_diag = CA_HWIN + CA_HALF
    u = np.arange(n_diag)
    idx = np.clip(u - (CA_HWIN - 1) + CA_PAD, -REL_CLIP, REL_CLIP) + REL_CLIP
    diag = rel_bias[:, idx].astype(F32)
    shifted = jnp.tile(diag, (1, CA_HWIN + 1))[:, :CA_HWIN * (n_diag + 1)].reshape(CA_HEADS, CA_HWIN, n_diag + 1)
    table = shifted[:, ::-1, :CA_HALF]
    return jnp.where(jnp.asarray(band)[None], table * LOG2E, NEG_INF)


def _diffattn_kernel(q_ref, k_ref, v_ref, lq1_ref, lk1_ref, lq2_ref, lk2_ref, linit_ref, ng_ref, o_ref,
                     vt_ref, s_ref, pv_ref):
    s_len = q_ref.shape[0]
    n_blocks = s_len // DA_BLOCK
    hw = 2 * DA_DH
    lam = (jnp.exp(jnp.sum(lq1_ref[...] * lk1_ref[...], axis=-1, keepdims=True))
           - jnp.exp(jnp.sum(lq2_ref[...] * lk2_ref[...], axis=-1, keepdims=True)) + linit_ref[...])
    out_scale = ng_ref[...] * (1.0 - linit_ref[...])
    for c in range(n_blocks):
        blk = slice(c * DA_BLOCK, (c + 1) * DA_BLOCK)
        vt_ref[0:hw, blk] = v_ref[blk, :].astype(F32).T.astype(BF16)
    vt_ref[hw:, :] = jnp.ones((ONES_ROWS, s_len), BF16)
    key_chunk = lax.broadcasted_iota(jnp.int32, (DA_BLOCK, DA_BLOCK), 0) // CHUNK
    query_chunk = lax.broadcasted_iota(jnp.int32, (DA_BLOCK, DA_BLOCK), 1) // CHUNK
    diag_allowed = key_chunk <= query_chunk
    lane = lax.broadcasted_iota(jnp.int32, (DA_BLOCK, hw), 1)
    head_lanes = (lane < DA_DH, lane >= DA_DH)

    steps = [(i, t, c) for i in range(n_blocks) for t in range(2) for c in range(i + 1)]
    maxima = {}

    def scores(n):
        i, t, c = steps[n]
        q = q_ref[i * DA_BLOCK:(i + 1) * DA_BLOCK, :]
        qz = jnp.where(head_lanes[t], q, jnp.zeros_like(q))
        s = _dot_nt(k_ref[c * DA_BLOCK:(c + 1) * DA_BLOCK, :], qz)
        if c == i:
            s = jnp.where(diag_allowed, s, NEG_INF)
        s_ref[n % DA_RING] = s
        maxima[n] = jnp.max(s, axis=0, keepdims=True)

    def attend(n):
        i, t, c = steps[n]
        p = jnp.exp2(s_ref[n % DA_RING] - maxima[n]).astype(BF16)
        pv_ref[i % 2, t, c] = _dot(vt_ref[:, c * DA_BLOCK:(c + 1) * DA_BLOCK], p)
        if c == i:
            first = n - i
            m = functools.reduce(jnp.maximum, [maxima[first + j] for j in range(i + 1)])
            acc = None
            for j in range(i + 1):
                part = pv_ref[i % 2, t, j] * jnp.exp2(maxima.pop(first + j) - m)
                acc = part if acc is None else acc + part
            heads[t] = acc[0:hw] * (1.0 / acc[hw:hw + 1])
            if t == 1:
                o_t = heads[0] - lam * heads[1]
                ms = jnp.mean(o_t * o_t, axis=0, keepdims=True)
                y = (o_t * lax.rsqrt(ms + LN_EPS)).T * out_scale
                o_ref[i * DA_BLOCK:(i + 1) * DA_BLOCK, :] = y.astype(BF16)

    heads = [None, None]
    for n in range(len(steps) + DA_AHEAD):
        if n < len(steps):
            scores(n)
        if n >= DA_AHEAD:
            attend(n - DA_AHEAD)


def _diffattn(h3, lams, norm_g, lambda_init):
    b, s, _ = h3.shape
    hw = 2 * DA_DH
    small = pl.BlockSpec((1, DA_DH), lambda i, h: (0, 0))
    return pl.pallas_call(
        _diffattn_kernel,
        grid=(b, DA_HEADS),
        in_specs=[pl.BlockSpec((None, s, hw), lambda i, h: (i, 0, OFF_DQ // hw + h)),
                  pl.BlockSpec((None, s, hw), lambda i, h: (i, 0, OFF_DK // hw + h)),
                  pl.BlockSpec((None, s, hw), lambda i, h: (i, 0, OFF_DV // hw + h)),
                  small, small, small, small,
                  pl.BlockSpec((1, 1), lambda i, h: (0, 0)),
                  pl.BlockSpec((1, hw), lambda i, h: (0, 0))],
        out_specs=pl.BlockSpec((None, s, hw), lambda i, h: (i, 0, h)),
        out_shape=jax.ShapeDtypeStruct((b, s, DA_V_W), BF16),
        scratch_shapes=[pltpu.VMEM((hw + ONES_ROWS, s), BF16),
                        pltpu.VMEM((DA_RING, DA_BLOCK, DA_BLOCK), F32),
                        pltpu.VMEM((2, 2, s // DA_BLOCK, hw + ONES_ROWS, DA_BLOCK), F32)],
        compiler_params=_params(2),
        name="diffattn",
    )(h3, h3, h3, *lams, jnp.full((1, 1), lambda_init, F32), norm_g)


def _layer_norm(y, g, b):
    mu = jnp.mean(y, axis=-1, keepdims=True)
    d = y - mu
    var = jnp.mean(d * d, axis=-1, keepdims=True)
    return d * lax.rsqrt(var + LN_EPS) * g + b


def _merge_kernel(layer_ref, x_ref, oa_ref, ob_ref, oc_ref, wg_ref, bg_ref, wa_ref, wb_ref, wc_ref, wo_ref, g_ref, b_ref,
                  o_ref):
    for r0 in range(0, x_ref.shape[0], SUB_ROWS):
        rows = slice(r0, r0 + SUB_ROWS)
        xf = x_ref[rows, :]
        xb = xf.astype(BF16)
        merged = None
        for n, (br_ref, w_ref) in enumerate(((oa_ref, wa_ref), (ob_ref, wb_ref), (oc_ref, wc_ref))):
            cols = slice(n * D_MODEL, (n + 1) * D_MODEL)
            z = _dot(xb, wg_ref[:, cols]) + bg_ref[:, cols]
            term = _dot(br_ref[rows, :], w_ref[...]) * (1.0 / (1.0 + jnp.exp(-z)))
            merged = term if merged is None else merged + term
        mix = _dot(merged.astype(BF16), wo_ref[...])
        o_ref[rows, :] = _layer_norm(DEEPNORM_ALPHA * xf + mix, g_ref[...], b_ref[...])


def _merge(layer, x2, oa, ob, oc, w_in, bg, wa, wb, wc, wo, g, b, tm):
    t = x2.shape[0]
    row = lambda w: pl.BlockSpec((tm, w), lambda i, l: (i, 0))
    return pl.pallas_call(
        _merge_kernel,
        grid_spec=pltpu.PrefetchScalarGridSpec(
            num_scalar_prefetch=1,
            grid=(t // tm,),
            in_specs=[row(D_MODEL), row(RET_V_W), row(CA_W), row(DA_V_W),
                      pl.BlockSpec((None, D_MODEL, GATE_W), lambda i, l: (l[0], 0, MIX_W // GATE_W),
                                   pipeline_mode=pl.Buffered(1)),
                      _const_spec((1, GATE_W)),
                      _const_spec((RET_V_W, D_MODEL)), _const_spec((CA_W, D_MODEL)), _const_spec((DA_V_W, D_MODEL)),
                      _const_spec((D_MODEL, D_MODEL)), _const_spec((1, D_MODEL)), _const_spec((1, D_MODEL))],
            out_specs=row(D_MODEL)),
        out_shape=jax.ShapeDtypeStruct((t, D_MODEL), F32),
        compiler_params=_params(1),
        name="merge",
    )(layer, x2, oa, ob, oc, w_in, bg, wa, wb, wc, wo, g, b)


FFN_CHUNK = 256


def _ffn_kernel(x_ref, wi_ref, wo_ref, g_ref, b_ref, o_ref, act_ref):
    for r0 in range(0, x_ref.shape[0], SUB_ROWS):
        rows = slice(r0, r0 + SUB_ROWS)
        xf = x_ref[rows, :]
        xb = xf.astype(BF16)
        for c in range(0, FFN_HIDDEN, FFN_CHUNK):
            ug = _dot(xb, wi_ref[:, c:c + FFN_CHUNK])
            uu = _dot(xb, wi_ref[:, FFN_HIDDEN + c:FFN_HIDDEN + c + FFN_CHUNK])
            act_ref[rows, c:c + FFN_CHUNK] = (ug / (1.0 + jnp.exp(-ug)) * uu).astype(BF16)
        ffn = _dot(act_ref[rows, :], wo_ref[...])
        o_ref[rows, :] = _layer_norm(DEEPNORM_ALPHA * xf + ffn, g_ref[...], b_ref[...])


def _ffn(x2, wi, wo, g, b, tm):
    t = x2.shape[0]
    return pl.pallas_call(
        _ffn_kernel,
        grid=(t // tm,),
        in_specs=[pl.BlockSpec((tm, D_MODEL), lambda i: (i, 0)),
                  _const_spec((D_MODEL, 2 * FFN_HIDDEN)), _const_spec((FFN_HIDDEN, D_MODEL)),
                  _const_spec((1, D_MODEL)), _const_spec((1, D_MODEL))],
        out_specs=pl.BlockSpec((tm, D_MODEL), lambda i: (i, 0)),
        out_shape=jax.ShapeDtypeStruct((t, D_MODEL), F32),
        scratch_shapes=[pltpu.VMEM((tm, FFN_HIDDEN), BF16)],
        compiler_params=_params(1),
        name="ffn",
    )(x2, wi, wo, g, b)


def _rotary_tables(seq):
    pos = jnp.arange(seq, dtype=F32)[:, None]

    def cs(d):
        inv_freq = ROPE_THETA ** (-jnp.arange(0, d, 2, dtype=F32) / d)
        ang = pos * inv_freq[None, :]
        return jnp.cos(ang), jnp.sin(ang)

    rc, rs = cs(RET_DK)
    rcos = jnp.concatenate([rc, rc], axis=1)
    rsin = jnp.concatenate([-rs, rs], axis=1)
    dc, ds = cs(DA_DH)
    z = jnp.zeros_like(ds)
    dcos = jnp.concatenate([dc, dc, dc, dc], axis=1)
    dsina = jnp.concatenate([-ds, z, -ds, z], axis=1)
    dsinb = jnp.concatenate([z, ds, z, ds], axis=1)
    return rcos, rsin, dcos, dsina, dsinb


def kernel(x, w_in, ret_norm_g, ca_rel_bias, da_lambda_q1, da_lambda_k1, da_lambda_q2, da_lambda_k2, da_norm_g, w_branch_a, w_branch_b, w_branch_c, b_merge, w_out, ln1_g, ln1_b, w_ffn_in, w_ffn_out, ln2_g, ln2_b):
    b, s, d = x.shape
    t = b * s
    rot_tabs = _rotary_tables(s)
    ret_tabs = _retention_tables()
    row = lambda a: a.reshape(1, -1).astype(F32)

    assert MIX_W % GATE_W == 0
    w_in_b = w_in.astype(BF16)
    x2 = x.reshape(t, d)
    for l in range(DEPTH):
        lambda_init = 0.8 - 0.6 * math.exp(-0.3 * l)
        layer = jnp.full((1,), l, jnp.int32)
        h = _inproj(layer, x2, w_in_b, rot_tabs, s, PROJ_ROWS)
        h3 = h.reshape(b, s, MIX_W)
        o_a = _retention(h3, ret_tabs, row(ret_norm_g[l]))
        o_b = _chunkattn(h3, _chunkattn_bias(ca_rel_bias[l]))
        o_c = _diffattn(h3, (row(da_lambda_q1[l]), row(da_lambda_k1[l]), row(da_lambda_q2[l]), row(da_lambda_k2[l])),
                        row(da_norm_g[l]), lambda_init)
        x2 = _merge(layer, x2, o_a.reshape(t, RET_V_W), o_b.reshape(t, CA_W), o_c.reshape(t, DA_V_W),
                    w_in_b, row(b_merge[l]), w_branch_a[l].astype(BF16), w_branch_b[l].astype(BF16),
                    w_branch_c[l].astype(BF16), w_out[l].astype(BF16), row(ln1_g[l]), row(ln1_b[l]), TAIL_ROWS)
        x2 = _ffn(x2, w_ffn_in[l].astype(BF16), w_ffn_out[l].astype(BF16), row(ln2_g[l]), row(ln2_b[l]), TAIL_ROWS)
    return x2.reshape(b, s, d)
```

```python
import functools
import math

import jax
import jax.numpy as jnp
import numpy as np
from jax import lax
from jax.experimental import pallas as pl
from jax.experimental.pallas import tpu as pltpu

D_MODEL = 1024
DEPTH = 4
CHUNK = 64
RET_HEADS, RET_DK, RET_DV = 4, 128, 256
CA_HEADS, CA_DH, CA_LEFT_CHUNKS, REL_CLIP = 8, 64, 8, 256
DA_HEADS, DA_DH = 4, 64
FFN_HIDDEN = -(-8 * D_MODEL // (3 * 256)) * 256
ROPE_THETA = 10000.0
LN_EPS = 1e-5
NEG_INF = -1e30
DEEPNORM_ALPHA = (2.0 * DEPTH) ** 0.25

RET_QK_W = RET_HEADS * RET_DK
RET_V_W = RET_HEADS * RET_DV
CA_W = CA_HEADS * CA_DH
DA_QK_W = 2 * DA_HEADS * DA_DH
DA_V_W = DA_HEADS * 2 * DA_DH
OFF_RQ = 0
OFF_RK = OFF_RQ + RET_QK_W
OFF_RV = OFF_RK + RET_QK_W
OFF_RG = OFF_RV + RET_V_W
OFF_CQ = OFF_RG + RET_V_W
OFF_CK = OFF_CQ + CA_W
OFF_CV = OFF_CK + CA_W
OFF_DQ = OFF_CV + CA_W
OFF_DK = OFF_DQ + DA_QK_W
OFF_DV = OFF_DK + DA_QK_W
MIX_W = OFF_DV + DA_V_W
GATE_W = 3 * D_MODEL

LOG2E = math.log2(math.e)
LANES = 128
ONES_ROWS = 16
VMEM_LIMIT = 56 * 1024 * 1024

RET_BLOCK = 256
CA_QBLOCK = 256
CA_HALF = CA_QBLOCK // 2
CA_PAD = CA_LEFT_CHUNKS * CHUNK
CA_WIN = CA_PAD + CA_QBLOCK
CA_HWIN = CA_PAD + CA_HALF
CA_AHEAD = 2
CA_DEPTH = 2 * CA_AHEAD
assert CA_HEADS % CA_DEPTH == 0
CA_SLAB = CA_DH + ONES_ROWS
DA_BLOCK = 256
DA_AHEAD = 6
DA_RING = DA_AHEAD + 1
SUB_ROWS = 256
PROJ_ROWS = 512
TAIL_ROWS = 4 * SUB_ROWS

BF16 = jnp.bfloat16
F32 = jnp.float32


def _dot(a, b):
    return jnp.dot(a, b, preferred_element_type=F32)


def _dot_nt(a, b):
    return lax.dot_general(a, b, (((1,), (1,)), ((), ())), preferred_element_type=F32)


def _dot_tn(a, b):
    return lax.dot_general(a, b, (((0,), (0,)), ((), ())), preferred_element_type=F32)


def _const_spec(shape):
    zeros = (0,) * len(shape)
    return pl.BlockSpec(shape, lambda *_: zeros, pipeline_mode=pl.Buffered(1))


def _params(n_axes):
    return pltpu.CompilerParams(dimension_semantics=("arbitrary",) * n_axes, vmem_limit_bytes=VMEM_LIMIT)


def _inproj_kernel(layer_ref, x_ref, w_ref, rcos_ref, rsin_ref, dcos_ref, dsina_ref, dsinb_ref, o_ref):
    xb = x_ref[...].astype(BF16)

    def proj(off, width):
        return _dot(xb, w_ref[:, off:off + width])

    def ret_rotary(a, scale):
        outs = []
        for h in range(RET_HEADS):
            ah = a[:, h * RET_DK:(h + 1) * RET_DK]
            r = ah * rcos_ref[...] + pltpu.roll(ah, RET_DK // 2, 1) * rsin_ref[...]
            outs.append(r * scale if scale != 1.0 else r)
        return jnp.concatenate(outs, axis=1)

    def da_rotary(a, scale):
        outs = []
        for g in range(DA_QK_W // LANES):
            ag = a[:, g * LANES:(g + 1) * LANES]
            r = (ag * dcos_ref[...] + pltpu.roll(ag, LANES - DA_DH // 2, 1) * dsina_ref[...]
                 + pltpu.roll(ag, DA_DH // 2, 1) * dsinb_ref[...])
            outs.append(r * scale if scale != 1.0 else r)
        return jnp.concatenate(outs, axis=1)

    o_ref[:, OFF_RQ:OFF_RQ + RET_QK_W] = ret_rotary(proj(OFF_RQ, RET_QK_W), 1.0).astype(BF16)
    o_ref[:, OFF_RK:OFF_RK + RET_QK_W] = ret_rotary(proj(OFF_RK, RET_QK_W), RET_DK ** -0.5).astype(BF16)
    for off in range(OFF_RV, OFF_CQ, 512):
        o_ref[:, off:off + 512] = proj(off, 512).astype(BF16)
    o_ref[:, OFF_CQ:OFF_CQ + CA_W] = (proj(OFF_CQ, CA_W) * (CA_DH ** -0.5 * LOG2E)).astype(BF16)
    o_ref[:, OFF_CK:OFF_CK + CA_W] = proj(OFF_CK, CA_W).astype(BF16)
    o_ref[:, OFF_CV:OFF_CV + CA_W] = proj(OFF_CV, CA_W).astype(BF16)
    o_ref[:, OFF_DQ:OFF_DQ + DA_QK_W] = da_rotary(proj(OFF_DQ, DA_QK_W), DA_DH ** -0.5 * LOG2E).astype(BF16)
    o_ref[:, OFF_DK:OFF_DK + DA_QK_W] = da_rotary(proj(OFF_DK, DA_QK_W), 1.0).astype(BF16)
    o_ref[:, OFF_DV:OFF_DV + DA_V_W] = proj(OFF_DV, DA_V_W).astype(BF16)


def _inproj(layer, x2, w_in, tabs, seq, tm):
    t = x2.shape[0]
    pos_blocks = seq // tm
    tab_spec = pl.BlockSpec((tm, LANES), lambda i, l: (i % pos_blocks, 0))
    return pl.pallas_call(
        _inproj_kernel,
        grid_spec=pltpu.PrefetchScalarGridSpec(
            num_scalar_prefetch=1,
            grid=(t // tm,),
            in_specs=[pl.BlockSpec((tm, D_MODEL), lambda i, l: (i, 0)),
                      pl.BlockSpec((None, D_MODEL, MIX_W), lambda i, l: (l[0], 0, 0), pipeline_mode=pl.Buffered(1)),
                      tab_spec, tab_spec, tab_spec, tab_spec, tab_spec],
            out_specs=pl.BlockSpec((tm, MIX_W), lambda i, l: (i, 0))),
        out_shape=jax.ShapeDtypeStruct((t, MIX_W), BF16),
        compiler_params=_params(1),
        name="inproj",
    )(layer, x2, w_in, *tabs)


def _retention_kernel(q_ref, k_ref, v_ref, g_ref, dmat_ref, qdec_ref, kdec_ref, cdec_ref, ng_ref, o_ref,
                      state_ref, sv_ref, kv_ref):
    n_blocks = q_ref.shape[0] // RET_BLOCK

    def local(h, c):
        rows = slice(c * RET_BLOCK, (c + 1) * RET_BLOCK)
        q = q_ref[rows, h * RET_DK:(h + 1) * RET_DK]
        k = k_ref[rows, h * RET_DK:(h + 1) * RET_DK]
        v = v_ref[rows, h * RET_DV:(h + 1) * RET_DV]
        s = _dot_nt(q, k) * dmat_ref[h]
        sv_ref[c % 2, h] = _dot(s.astype(BF16), v)
        kd = (k.astype(F32) * kdec_ref[h]).astype(BF16)
        kv_ref[c % 2, h] = _dot_tn(kd, v)

    def finish(h, c):
        rows = slice(c * RET_BLOCK, (c + 1) * RET_BLOCK)
        cols = slice(h * RET_DV, (h + 1) * RET_DV)
        o = sv_ref[c % 2, h]
        if c > 0:
            qd = (q_ref[rows, h * RET_DK:(h + 1) * RET_DK].astype(F32) * qdec_ref[h]).astype(BF16)
            o = o + _dot(qd, state_ref[h].astype(BF16))
            state_ref[h] = state_ref[h] * cdec_ref[h, 0:1, :] + kv_ref[c % 2, h]
        else:
            state_ref[h] = kv_ref[c % 2, h]
        ms = jnp.mean(o * o, axis=-1, keepdims=True)
        y = o * lax.rsqrt(ms + LN_EPS) * ng_ref[...]
        gate = g_ref[rows, cols].astype(F32)
        o_ref[rows, cols] = (gate / (1.0 + jnp.exp(-gate)) * y).astype(BF16)

    for h in range(RET_HEADS):
        local(h, 0)
    for c in range(n_blocks):
        for h in range(RET_HEADS):
            if c + 1 < n_blocks:
                local(h, c + 1)
            finish(h, c)


def _retention(h3, tabs, norm_g):
    b, s, _ = h3.shape
    dmat, qdec, kdec, cdec = tabs
    return pl.pallas_call(
        _retention_kernel,
        grid=(b,),
        in_specs=[pl.BlockSpec((None, s, RET_QK_W), lambda i: (i, 0, OFF_RQ // RET_QK_W)),
                  pl.BlockSpec((None, s, RET_QK_W), lambda i: (i, 0, OFF_RK // RET_QK_W)),
                  pl.BlockSpec((None, s, RET_V_W), lambda i: (i, 0, OFF_RV // RET_V_W)),
                  pl.BlockSpec((None, s, RET_V_W), lambda i: (i, 0, OFF_RG // RET_V_W)),
                  _const_spec((RET_HEADS, RET_BLOCK, RET_BLOCK)),
                  _const_spec((RET_HEADS, RET_BLOCK, RET_DK)),
                  _const_spec((RET_HEADS, RET_BLOCK, RET_DK)),
                  _const_spec((RET_HEADS, 8, RET_DV)),
                  _const_spec((1, RET_DV))],
        out_specs=pl.BlockSpec((None, s, RET_V_W), lambda i: (i, 0, 0)),
        out_shape=jax.ShapeDtypeStruct((b, s, RET_V_W), BF16),
        scratch_shapes=[pltpu.VMEM((RET_HEADS, RET_DK, RET_DV), F32),
                        pltpu.VMEM((2, RET_HEADS, RET_BLOCK, RET_DV), F32),
                        pltpu.VMEM((2, RET_HEADS, RET_DK, RET_DV), F32)],
        compiler_params=_params(1),
        name="retention",
    )(h3, h3, h3, h3, dmat, qdec, kdec, cdec, norm_g)


def _retention_tables():
    h = np.arange(RET_HEADS, dtype=np.float64)
    log_gamma = np.log1p(-np.exp2(-5.0 - h))
    n = np.arange(RET_BLOCK)
    diff = (n[:, None] - n[None, :]).astype(np.float64)
    cn, cm = n[:, None] // CHUNK, n[None, :] // CHUNK
    expo = np.where(cm == cn, np.abs(diff), diff)
    dmat = np.where(cm <= cn, np.exp(log_gamma[:, None, None] * expo[None]), 0.0)
    qdec = np.exp(log_gamma[:, None] * (n[None, :] + 1.0))
    kdec = np.exp(log_gamma[:, None] * (RET_BLOCK - 1.0 - n[None, :]))
    cdec = np.exp(log_gamma * RET_BLOCK)
    qdec = np.broadcast_to(qdec[:, :, None], (RET_HEADS, RET_BLOCK, RET_DK))
    kdec = np.broadcast_to(kdec[:, :, None], (RET_HEADS, RET_BLOCK, RET_DK))
    cdec = np.broadcast_to(cdec[:, None, None], (RET_HEADS, 8, RET_DV))
    return tuple(jnp.asarray(a, F32) for a in (dmat, qdec, kdec, cdec))


def _chunkattn_kernel(q_ref, k_ref, v_ref, bias_ref, o_ref, kpad_ref, vt_ref, s_ref, p_ref, ot_ref):
    s_len = q_ref.shape[0]
    n_blocks = s_len // CA_QBLOCK
    n_pad = CA_PAD // CA_QBLOCK
    kpad_ref[0:CA_PAD, :] = jnp.zeros((CA_PAD, CA_W), BF16)
    kpad_ref[CA_PAD:, :] = k_ref[...]
    ones = jnp.ones((ONES_ROWS, CA_QBLOCK), BF16)
    for c in range(n_pad + n_blocks):
        if c < n_pad:
            vt = jnp.zeros((CA_W, CA_QBLOCK), BF16)
        else:
            vt = v_ref[(c - n_pad) * CA_QBLOCK:(c - n_pad + 1) * CA_QBLOCK, :].astype(F32).T.astype(BF16)
        for h in range(CA_HEADS):
            vt_ref[c, h * CA_SLAB:h * CA_SLAB + CA_DH, :] = vt[h * CA_DH:(h + 1) * CA_DH, :]
            vt_ref[c, h * CA_SLAB + CA_DH:(h + 1) * CA_SLAB, :] = ones
    p_ref[:, CA_HWIN:CA_WIN, 0:CA_HALF] = jnp.zeros((2, CA_WIN - CA_HWIN, CA_HALF), BF16)
    p_ref[:, 0:CA_HALF, CA_HALF:CA_QBLOCK] = jnp.zeros((2, CA_HALF, CA_HALF), BF16)
    lane = lax.broadcasted_iota(jnp.int32, (CA_QBLOCK, LANES), 1)
    head_lanes = (lane < CA_DH, lane >= CA_DH)
    win_row = lax.broadcasted_iota(jnp.int32, (CA_HWIN, CA_HALF), 0)

    def row0(i):
        return i * CA_QBLOCK if isinstance(i, int) else pl.multiple_of(i * CA_QBLOCK, CA_QBLOCK)

    def scores(i, h):
        pair = slice((h // 2) * LANES, (h // 2 + 1) * LANES)
        q = q_ref[pl.ds(row0(i), CA_QBLOCK), pair]
        qz = jnp.where(head_lanes[h % 2], q, jnp.zeros_like(q))
        s_ref[h % CA_DEPTH] = _dot_nt(kpad_ref[pl.ds(row0(i), CA_WIN), pair], qz)

    def attend(i, h, first_key):
        for half in range(2):
            rows = slice(half * CA_HALF, half * CA_HALF + CA_HWIN)
            cols = slice(half * CA_HALF, (half + 1) * CA_HALF)
            s = s_ref[h % CA_DEPTH, rows, cols] + bias_ref[h]
            if first_key is not None:
                s = jnp.where(win_row >= -(first_key + half * CA_HALF), s, NEG_INF)
            m = jnp.max(s, axis=0, keepdims=True)
            p_ref[h % 2, rows, cols] = jnp.exp2(s - m).astype(BF16)
        pv = None
        for c in range(CA_WIN // CA_QBLOCK):
            part = _dot(vt_ref[i + c, h * CA_SLAB:(h + 1) * CA_SLAB, :],
                        p_ref[h % 2, c * CA_QBLOCK:(c + 1) * CA_QBLOCK, :])
            pv = part if pv is None else pv + part
        ot_ref[h * CA_DH:(h + 1) * CA_DH, :] = pv[0:CA_DH] * (1.0 / pv[CA_DH:CA_DH + 1])

    def qblock(i, first_key, nxt):
        for h in range(CA_HEADS):
            ahead = h + CA_AHEAD
            if ahead < CA_HEADS:
                scores(i, ahead)
            elif nxt is not None:
                scores(nxt, ahead - CA_HEADS)
            attend(i, h, first_key)
        o_ref[pl.ds(row0(i), CA_QBLOCK), :] = ot_ref[...].T.astype(BF16)

    for h in range(CA_AHEAD):
        scores(0, h)
    for i in range(n_pad):
        qblock(i, i * CA_QBLOCK - CA_PAD, i + 1)

    def body(i, carry):
        qblock(i, None, i + 1)
        return carry

    lax.fori_loop(n_pad, n_blocks - 1, body, 0)
    qblock(n_blocks - 1, None, None)


def _chunkattn(h3, bias):
    b, s, _ = h3.shape
    n_blocks = s // CA_QBLOCK
    return pl.pallas_call(
        _chunkattn_kernel,
        grid=(b,),
        in_specs=[pl.BlockSpec((None, s, CA_W), lambda i: (i, 0, OFF_CQ // CA_W)),
                  pl.BlockSpec((None, s, CA_W), lambda i: (i, 0, OFF_CK // CA_W)),
                  pl.BlockSpec((None, s, CA_W), lambda i: (i, 0, OFF_CV // CA_W)),
                  _const_spec((CA_HEADS, CA_HWIN, CA_HALF))],
        out_specs=pl.BlockSpec((None, s, CA_W), lambda i: (i, 0, 0)),
        out_shape=jax.ShapeDtypeStruct((b, s, CA_W), BF16),
        scratch_shapes=[pltpu.VMEM((s + CA_PAD, CA_W), BF16),
                        pltpu.VMEM((n_blocks + CA_PAD // CA_QBLOCK, CA_HEADS * CA_SLAB, CA_QBLOCK), BF16),
                        pltpu.VMEM((CA_DEPTH, CA_WIN, CA_QBLOCK), F32),
                        pltpu.VMEM((2, CA_WIN, CA_QBLOCK), BF16),
                        pltpu.VMEM((CA_W, CA_QBLOCK), F32)],
        compiler_params=_params(1),
        name="chunkattn",
    )(h3, h3, h3, bias)


def _chunkattn_bias(rel_bias):
    j = np.arange(CA_HWIN)[:, None]
    r = np.arange(CA_HALF)[None, :]
    qc, kc = r // CHUNK, j // CHUNK
    band = (kc >= qc) & (kc <= qc + CA_LEFT_CHUNKS)
    n_diag = CA_HWIN + CA_HALF
    u = np.arange(n_diag)
    d = np.where(u < CA_HALF, u, u - n_diag)
    idx = np.clip(d + CA_PAD, -REL_CLIP, REL_CLIP) + REL_CLIP
    diag = rel_bias[:, idx].astype(F32) * LOG2E
    shifted = jnp.tile(diag, (1, CA_HWIN))[:, :CA_HWIN * (n_diag - 1)].reshape(CA_HEADS, CA_HWIN, n_diag - 1)
    table = shifted[:, :, :CA_HALF]
    return jnp.where(jnp.asarray(band)[None], table, NEG_INF)


def _diffattn_kernel(q_ref, k_ref, v_ref, lq1_ref, lk1_ref, lq2_ref, lk2_ref, linit_ref, ng_ref, o_ref,
                     vt_ref, s_ref, pv_ref):
    s_len = q_ref.shape[0]
    n_blocks = s_len // DA_BLOCK
    hw = 2 * DA_DH
    lam = (jnp.exp(jnp.sum(lq1_ref[...] * lk1_ref[...], axis=-1, keepdims=True))
           - jnp.exp(jnp.sum(lq2_ref[...] * lk2_ref[...], axis=-1, keepdims=True)) + linit_ref[...])
    out_scale = ng_ref[...] * (1.0 - linit_ref[...])
    for c in range(n_blocks):
        blk = slice(c * DA_BLOCK, (c + 1) * DA_BLOCK)
        vt_ref[0:hw, blk] = v_ref[blk, :].astype(F32).T.astype(BF16)
    vt_ref[hw:, :] = jnp.ones((ONES_ROWS, s_len), BF16)
    key_chunk = lax.broadcasted_iota(jnp.int32, (DA_BLOCK, DA_BLOCK), 0) // CHUNK
    query_chunk = lax.broadcasted_iota(jnp.int32, (DA_BLOCK, DA_BLOCK), 1) // CHUNK
    diag_allowed = key_chunk <= query_chunk
    lane = lax.broadcasted_iota(jnp.int32, (DA_BLOCK, hw), 1)
    head_lanes = (lane < DA_DH, lane >= DA_DH)

    steps = [(i, t, c) for i in range(n_blocks) for t in range(2) for c in range(i + 1)]
    maxima = {}

    def scores(n):
        i, t, c = steps[n]
        q = q_ref[i * DA_BLOCK:(i + 1) * DA_BLOCK, :]
        qz = jnp.where(head_lanes[t], q, jnp.zeros_like(q))
        s = _dot_nt(k_ref[c * DA_BLOCK:(c + 1) * DA_BLOCK, :], qz)
        if c == i:
            s = jnp.where(diag_allowed, s, NEG_INF)
        s_ref[n % DA_RING] = s
        maxima[n] = jnp.max(s, axis=0, keepdims=True)

    def attend(n):
        i, t, c = steps[n]
        p = jnp.exp2(s_ref[n % DA_RING] - maxima[n]).astype(BF16)
        pv_ref[i % 2, t, c] = _dot(vt_ref[:, c * DA_BLOCK:(c + 1) * DA_BLOCK], p)
        if c == i:
            first = n - i
            m = functools.reduce(jnp.maximum, [maxima[first + j] for j in range(i + 1)])
            acc = None
            for j in range(i + 1):
                part = pv_ref[i % 2, t, j] * jnp.exp2(maxima.pop(first + j) - m)
                acc = part if acc is None else acc + part
            heads[t] = acc[0:hw] * (1.0 / acc[hw:hw + 1])
            if t == 1:
                o_t = heads[0] - lam * heads[1]
                ms = jnp.mean(o_t * o_t, axis=0, keepdims=True)
                y = (o_t * lax.rsqrt(ms + LN_EPS)).T * out_scale
                o_ref[i * DA_BLOCK:(i + 1) * DA_BLOCK, :] = y.astype(BF16)

    heads = [None, None]
    for n in range(len(steps) + DA_AHEAD):
        if n < len(steps):
            scores(n)
        if n >= DA_AHEAD:
            attend(n - DA_AHEAD)


def _diffattn(h3, lams, norm_g, lambda_init):
    b, s, _ = h3.shape
    hw = 2 * DA_DH
    small = pl.BlockSpec((1, DA_DH), lambda i, h: (0, 0))
    return pl.pallas_call(
        _diffattn_kernel,
        grid=(b, DA_HEADS),
        in_specs=[pl.BlockSpec((None, s, hw), lambda i, h: (i, 0, OFF_DQ // hw + h)),
                  pl.BlockSpec((None, s, hw), lambda i, h: (i, 0, OFF_DK // hw + h)),
                  pl.BlockSpec((None, s, hw), lambda i, h: (i, 0, OFF_DV // hw + h)),
                  small, small, small, small,
                  pl.BlockSpec((1, 1), lambda i, h: (0, 0)),
                  pl.BlockSpec((1, hw), lambda i, h: (0, 0))],
        out_specs=pl.BlockSpec((None, s, hw), lambda i, h: (i, 0, h)),
        out_shape=jax.ShapeDtypeStruct((b, s, DA_V_W), BF16),
        scratch_shapes=[pltpu.VMEM((hw + ONES_ROWS, s), BF16),
                        pltpu.VMEM((DA_RING, DA_BLOCK, DA_BLOCK), F32),
                        pltpu.VMEM((2, 2, s // DA_BLOCK, hw + ONES_ROWS, DA_BLOCK), F32)],
        compiler_params=_params(2),
        name="diffattn",
    )(h3, h3, h3, *lams, jnp.full((1, 1), lambda_init, F32), norm_g)


def _layer_norm(y, g, b):
    mu = jnp.mean(y, axis=-1, keepdims=True)
    d = y - mu
    var = jnp.mean(d * d, axis=-1, keepdims=True)
    return d * lax.rsqrt(var + LN_EPS) * g + b


def _merge_kernel(layer_ref, x_ref, oa_ref, ob_ref, oc_ref, wg_ref, bg_ref, wa_ref, wb_ref, wc_ref, wo_ref, g_ref, b_ref,
                  o_ref):
    for r0 in range(0, x_ref.shape[0], SUB_ROWS):
        rows = slice(r0, r0 + SUB_ROWS)
        xf = x_ref[rows, :]
        xb = xf.astype(BF16)
        merged = None
        for n, (br_ref, w_ref) in enumerate(((oa_ref, wa_ref), (ob_ref, wb_ref), (oc_ref, wc_ref))):
            cols = slice(n * D_MODEL, (n + 1) * D_MODEL)
            z = _dot(xb, wg_ref[:, cols]) + bg_ref[:, cols]
            term = _dot(br_ref[rows, :], w_ref[...]) * (1.0 / (1.0 + jnp.exp(-z)))
            merged = term if merged is None else merged + term
        mix = _dot(merged.astype(BF16), wo_ref[...])
        o_ref[rows, :] = _layer_norm(DEEPNORM_ALPHA * xf + mix, g_ref[...], b_ref[...])


def _merge(layer, x2, oa, ob, oc, w_in, bg, wa, wb, wc, wo, g, b, tm):
    t = x2.shape[0]
    row = lambda w: pl.BlockSpec((tm, w), lambda i, l: (i, 0))
    return pl.pallas_call(
        _merge_kernel,
        grid_spec=pltpu.PrefetchScalarGridSpec(
            num_scalar_prefetch=1,
            grid=(t // tm,),
            in_specs=[row(D_MODEL), row(RET_V_W), row(CA_W), row(DA_V_W),
                      pl.BlockSpec((None, D_MODEL, GATE_W), lambda i, l: (l[0], 0, MIX_W // GATE_W),
                                   pipeline_mode=pl.Buffered(1)),
                      _const_spec((1, GATE_W)),
                      _const_spec((RET_V_W, D_MODEL)), _const_spec((CA_W, D_MODEL)), _const_spec((DA_V_W, D_MODEL)),
                      _const_spec((D_MODEL, D_MODEL)), _const_spec((1, D_MODEL)), _const_spec((1, D_MODEL))],
            out_specs=row(D_MODEL)),
        out_shape=jax.ShapeDtypeStruct((t, D_MODEL), F32),
        compiler_params=_params(1),
        name="merge",
    )(layer, x2, oa, ob, oc, w_in, bg, wa, wb, wc, wo, g, b)


FFN_CHUNK = 256


def _ffn_kernel(x_ref, wi_ref, wo_ref, g_ref, b_ref, o_ref, act_ref):
    for r0 in range(0, x_ref.shape[0], SUB_ROWS):
        rows = slice(r0, r0 + SUB_ROWS)
        xf = x_ref[rows, :]
        xb = xf.astype(BF16)
        for c in range(0, FFN_HIDDEN, FFN_CHUNK):
            ug = _dot(xb, wi_ref[:, c:c + FFN_CHUNK])
            uu = _dot(xb, wi_ref[:, FFN_HIDDEN + c:FFN_HIDDEN + c + FFN_CHUNK])
            act_ref[rows, c:c + FFN_CHUNK] = (ug / (1.0 + jnp.exp(-ug)) * uu).astype(BF16)
        ffn = _dot(act_ref[rows, :], wo_ref[...])
        o_ref[rows, :] = _layer_norm(DEEPNORM_ALPHA * xf + ffn, g_ref[...], b_ref[...])


def _ffn(x2, wi, wo, g, b, tm):
    t = x2.shape[0]
    return pl.pallas_call(
        _ffn_kernel,
        grid=(t // tm,),
        in_specs=[pl.BlockSpec((tm, D_MODEL), lambda i: (i, 0)),
                  _const_spec((D_MODEL, 2 * FFN_HIDDEN)), _const_spec((FFN_HIDDEN, D_MODEL)),
                  _const_spec((1, D_MODEL)), _const_spec((1, D_MODEL))],
        out_specs=pl.BlockSpec((tm, D_MODEL), lambda i: (i, 0)),
        out_shape=jax.ShapeDtypeStruct((t, D_MODEL), F32),
        scratch_shapes=[pltpu.VMEM((tm, FFN_HIDDEN), BF16)],
        compiler_params=_params(1),
        name="ffn",
    )(x2, wi, wo, g, b)


def _rotary_tables(seq):
    pos = jnp.arange(seq, dtype=F32)[:, None]

    def cs(d):
        inv_freq = ROPE_THETA ** (-jnp.arange(0, d, 2, dtype=F32) / d)
        ang = pos * inv_freq[None, :]
        return jnp.cos(ang), jnp.sin(ang)

    rc, rs = cs(RET_DK)
    rcos = jnp.concatenate([rc, rc], axis=1)
    rsin = jnp.concatenate([-rs, rs], axis=1)
    dc, ds = cs(DA_DH)
    z = jnp.zeros_like(ds)
    dcos = jnp.concatenate([dc, dc, dc, dc], axis=1)
    dsina = jnp.concatenate([-ds, z, -ds, z], axis=1)
    dsinb = jnp.concatenate([z, ds, z, ds], axis=1)
    return rcos, rsin, dcos, dsina, dsinb


def kernel(x, w_in, ret_norm_g, ca_rel_bias, da_lambda_q1, da_lambda_k1, da_lambda_q2, da_lambda_k2, da_norm_g, w_branch_a, w_branch_b, w_branch_c, b_merge, w_out, ln1_g, ln1_b, w_ffn_in, w_ffn_out, ln2_g, ln2_b):
    b, s, d = x.shape
    t = b * s
    rot_tabs = _rotary_tables(s)
    ret_tabs = _retention_tables()
    row = lambda a: a.reshape(1, -1).astype(F32)

    assert MIX_W % GATE_W == 0
    w_in_b = w_in.astype(BF16)
    x2 = x.reshape(t, d)
    for l in range(DEPTH):
        lambda_init = 0.8 - 0.6 * math.exp(-0.3 * l)
        layer = jnp.full((1,), l, jnp.int32)
        h = _inproj(layer, x2, w_in_b, rot_tabs, s, PROJ_ROWS)
        h3 = h.reshape(b, s, MIX_W)
        o_a = _retention(h3, ret_tabs, row(ret_norm_g[l]))
        o_b = _chunkattn(h3, _chunkattn_bias(ca_rel_bias[l]))
        o_c = _diffattn(h3, (row(da_lambda_q1[l]), row(da_lambda_k1[l]), row(da_lambda_q2[l]), row(da_lambda_k2[l])),
                        row(da_norm_g[l]), lambda_init)
        x2 = _merge(layer, x2, o_a.reshape(t, RET_V_W), o_b.reshape(t, CA_W), o_c.reshape(t, DA_V_W),
                    w_in_b, row(b_merge[l]), w_branch_a[l].astype(BF16), w_branch_b[l].astype(BF16),
                    w_branch_c[l].astype(BF16), w_out[l].astype(BF16), row(ln1_g[l]), row(ln1_b[l]), TAIL_ROWS)
        x2 = _ffn(x2, w_ffn_in[l].astype(BF16), w_ffn_out[l].astype(BF16), row(ln2_g[l]), row(ln2_b[l]), TAIL_ROWS)
    return x2.reshape(b, s, d)
```

```python
import functools
import math

import jax
import jax.numpy as jnp
import numpy as np
from jax import lax
from jax.experimental import pallas as pl
from jax.experimental.pallas import tpu as pltpu

D_MODEL = 1024
DEPTH = 4
CHUNK = 64
RET_HEADS, RET_DK, RET_DV = 4, 128, 256
CA_HEADS, CA_DH, CA_LEFT_CHUNKS, REL_CLIP = 8, 64, 8, 256
DA_HEADS, DA_DH = 4, 64
FFN_HIDDEN = -(-8 * D_MODEL // (3 * 256)) * 256
ROPE_THETA = 10000.0
LN_EPS = 1e-5
NEG_INF = -1e30
DEEPNORM_ALPHA = (2.0 * DEPTH) ** 0.25

RET_QK_W = RET_HEADS * RET_DK
RET_V_W = RET_HEADS * RET_DV
CA_W = CA_HEADS * CA_DH
DA_QK_W = 2 * DA_HEADS * DA_DH
DA_V_W = DA_HEADS * 2 * DA_DH
OFF_RQ = 0
OFF_RK = OFF_RQ + RET_QK_W
OFF_RV = OFF_RK + RET_QK_W
OFF_RG = OFF_RV + RET_V_W
OFF_CQ = OFF_RG + RET_V_W
OFF_CK = OFF_CQ + CA_W
OFF_CV = OFF_CK + CA_W
OFF_DQ = OFF_CV + CA_W
OFF_DK = OFF_DQ + DA_QK_W
OFF_DV = OFF_DK + DA_QK_W
MIX_W = OFF_DV + DA_V_W
GATE_W = 3 * D_MODEL

LOG2E = math.log2(math.e)
LANES = 128
ONES_ROWS = 16
VMEM_LIMIT = 56 * 1024 * 1024

RET_BLOCK = 256
CA_QBLOCK = 256
CA_HALF = CA_QBLOCK // 2
CA_PAD = CA_LEFT_CHUNKS * CHUNK
CA_WIN = CA_PAD + CA_QBLOCK
CA_HWIN = CA_PAD + CA_HALF
CA_AHEAD = 2
CA_DEPTH = 2 * CA_AHEAD
assert CA_HEADS % CA_DEPTH == 0
CA_SLAB = CA_DH + ONES_ROWS
DA_BLOCK = 256
DA_PAIRS = 2
DA_AHEAD = 6 * DA_PAIRS
DA_RING = DA_AHEAD + 1
SUB_ROWS = 256
PROJ_ROWS = 512
TAIL_ROWS = 4 * SUB_ROWS

BF16 = jnp.bfloat16
F32 = jnp.float32


def _dot(a, b):
    return jnp.dot(a, b, preferred_element_type=F32)


def _dot_nt(a, b):
    return lax.dot_general(a, b, (((1,), (1,)), ((), ())), preferred_element_type=F32)


def _dot_tn(a, b):
    return lax.dot_general(a, b, (((0,), (0,)), ((), ())), preferred_element_type=F32)


def _const_spec(shape):
    zeros = (0,) * len(shape)
    return pl.BlockSpec(shape, lambda *_: zeros, pipeline_mode=pl.Buffered(1))


def _params(n_axes):
    return pltpu.CompilerParams(dimension_semantics=("arbitrary",) * n_axes, vmem_limit_bytes=VMEM_LIMIT)


def _inproj_kernel(layer_ref, x_ref, w_ref, rcos_ref, rsin_ref, dcos_ref, dsina_ref, dsinb_ref, o_ref):
    xb = x_ref[...].astype(BF16)

    def proj(off, width):
        return _dot(xb, w_ref[:, off:off + width])

    def ret_rotary(a, scale):
        outs = []
        for h in range(RET_HEADS):
            ah = a[:, h * RET_DK:(h + 1) * RET_DK]
            r = ah * rcos_ref[...] + pltpu.roll(ah, RET_DK // 2, 1) * rsin_ref[...]
            outs.append(r * scale if scale != 1.0 else r)
        return jnp.concatenate(outs, axis=1)

    def da_rotary(a, scale):
        outs = []
        for g in range(DA_QK_W // LANES):
            ag = a[:, g * LANES:(g + 1) * LANES]
            r = (ag * dcos_ref[...] + pltpu.roll(ag, LANES - DA_DH // 2, 1) * dsina_ref[...]
                 + pltpu.roll(ag, DA_DH // 2, 1) * dsinb_ref[...])
            outs.append(r * scale if scale != 1.0 else r)
        return jnp.concatenate(outs, axis=1)

    o_ref[:, OFF_RQ:OFF_RQ + RET_QK_W] = ret_rotary(proj(OFF_RQ, RET_QK_W), 1.0).astype(BF16)
    o_ref[:, OFF_RK:OFF_RK + RET_QK_W] = ret_rotary(proj(OFF_RK, RET_QK_W), RET_DK ** -0.5).astype(BF16)
    for off in range(OFF_RV, OFF_CQ, 512):
        o_ref[:, off:off + 512] = proj(off, 512).astype(BF16)
    o_ref[:, OFF_CQ:OFF_CQ + CA_W] = (proj(OFF_CQ, CA_W) * (CA_DH ** -0.5 * LOG2E)).astype(BF16)
    o_ref[:, OFF_CK:OFF_CK + CA_W] = proj(OFF_CK, CA_W).astype(BF16)
    o_ref[:, OFF_CV:OFF_CV + CA_W] = proj(OFF_CV, CA_W).astype(BF16)
    o_ref[:, OFF_DQ:OFF_DQ + DA_QK_W] = da_rotary(proj(OFF_DQ, DA_QK_W), DA_DH ** -0.5 * LOG2E).astype(BF16)
    o_ref[:, OFF_DK:OFF_DK + DA_QK_W] = da_rotary(proj(OFF_DK, DA_QK_W), 1.0).astype(BF16)
    o_ref[:, OFF_DV:OFF_DV + DA_V_W] = proj(OFF_DV, DA_V_W).astype(BF16)


def _inproj(layer, x2, w_in, tabs, seq, tm):
    t = x2.shape[0]
    pos_blocks = seq // tm
    tab_spec = pl.BlockSpec((tm, LANES), lambda i, l: (i % pos_blocks, 0))
    return pl.pallas_call(
        _inproj_kernel,
        grid_spec=pltpu.PrefetchScalarGridSpec(
            num_scalar_prefetch=1,
            grid=(t // tm,),
            in_specs=[pl.BlockSpec((tm, D_MODEL), lambda i, l: (i, 0)),
                      pl.BlockSpec((None, D_MODEL, MIX_W), lambda i, l: (l[0], 0, 0), pipeline_mode=pl.Buffered(1)),
                      tab_spec, tab_spec, tab_spec, tab_spec, tab_spec],
            out_specs=pl.BlockSpec((tm, MIX_W), lambda i, l: (i, 0))),
        out_shape=jax.ShapeDtypeStruct((t, MIX_W), BF16),
        compiler_params=_params(1),
        name="inproj",
    )(layer, x2, w_in, *tabs)


def _retention_kernel(q_ref, k_ref, v_ref, g_ref, dmat_ref, qdec_ref, kdec_ref, cdec_ref, ng_ref, o_ref,
                      state_ref, sv_ref, kv_ref):
    n_blocks = q_ref.shape[0] // RET_BLOCK

    def local(h, c):
        rows = slice(c * RET_BLOCK, (c + 1) * RET_BLOCK)
        q = q_ref[rows, h * RET_DK:(h + 1) * RET_DK]
        k = k_ref[rows, h * RET_DK:(h + 1) * RET_DK]
        v = v_ref[rows, h * RET_DV:(h + 1) * RET_DV]
        s = _dot_nt(q, k) * dmat_ref[h]
        sv_ref[c % 2, h] = _dot(s.astype(BF16), v)
        kd = (k.astype(F32) * kdec_ref[h]).astype(BF16)
        kv_ref[c % 2, h] = _dot_tn(kd, v)

    def finish(h, c):
        rows = slice(c * RET_BLOCK, (c + 1) * RET_BLOCK)
        cols = slice(h * RET_DV, (h + 1) * RET_DV)
        o = sv_ref[c % 2, h]
        if c > 0:
            qd = (q_ref[rows, h * RET_DK:(h + 1) * RET_DK].astype(F32) * qdec_ref[h]).astype(BF16)
            o = o + _dot(qd, state_ref[h].astype(BF16))
            state_ref[h] = state_ref[h] * cdec_ref[h, 0:1, :] + kv_ref[c % 2, h]
        else:
            state_ref[h] = kv_ref[c % 2, h]
        ms = jnp.mean(o * o, axis=-1, keepdims=True)
        y = o * lax.rsqrt(ms + LN_EPS) * ng_ref[...]
        gate = g_ref[rows, cols].astype(F32)
        o_ref[rows, cols] = (gate / (1.0 + jnp.exp(-gate)) * y).astype(BF16)

    for h in range(RET_HEADS):
        local(h, 0)
    for c in range(n_blocks):
        for h in range(RET_HEADS):
            if c + 1 < n_blocks:
                local(h, c + 1)
            finish(h, c)


def _retention(h3, tabs, norm_g):
    b, s, _ = h3.shape
    dmat, qdec, kdec, cdec = tabs
    return pl.pallas_call(
        _retention_kernel,
        grid=(b,),
        in_specs=[pl.BlockSpec((None, s, RET_QK_W), lambda i: (i, 0, OFF_RQ // RET_QK_W)),
                  pl.BlockSpec((None, s, RET_QK_W), lambda i: (i, 0, OFF_RK // RET_QK_W)),
                  pl.BlockSpec((None, s, RET_V_W), lambda i: (i, 0, OFF_RV // RET_V_W)),
                  pl.BlockSpec((None, s, RET_V_W), lambda i: (i, 0, OFF_RG // RET_V_W)),
                  _const_spec((RET_HEADS, RET_BLOCK, RET_BLOCK)),
                  _const_spec((RET_HEADS, RET_BLOCK, RET_DK)),
                  _const_spec((RET_HEADS, RET_BLOCK, RET_DK)),
                  _const_spec((RET_HEADS, 8, RET_DV)),
                  _const_spec((1, RET_DV))],
        out_specs=pl.BlockSpec((None, s, RET_V_W), lambda i: (i, 0, 0)),
        out_shape=jax.ShapeDtypeStruct((b, s, RET_V_W), BF16),
        scratch_shapes=[pltpu.VMEM((RET_HEADS, RET_DK, RET_DV), F32),
                        pltpu.VMEM((2, RET_HEADS, RET_BLOCK, RET_DV), F32),
                        pltpu.VMEM((2, RET_HEADS, RET_DK, RET_DV), F32)],
        compiler_params=_params(1),
        name="retention",
    )(h3, h3, h3, h3, dmat, qdec, kdec, cdec, norm_g)


def _retention_tables():
    h = np.arange(RET_HEADS, dtype=np.float64)
    log_gamma = np.log1p(-np.exp2(-5.0 - h))
    n = np.arange(RET_BLOCK)
    diff = (n[:, None] - n[None, :]).astype(np.float64)
    cn, cm = n[:, None] // CHUNK, n[None, :] // CHUNK
    expo = np.where(cm == cn, np.abs(diff), diff)
    dmat = np.where(cm <= cn, np.exp(log_gamma[:, None, None] * expo[None]), 0.0)
    qdec = np.exp(log_gamma[:, None] * (n[None, :] + 1.0))
    kdec = np.exp(log_gamma[:, None] * (RET_BLOCK - 1.0 - n[None, :]))
    cdec = np.exp(log_gamma * RET_BLOCK)
    qdec = np.broadcast_to(qdec[:, :, None], (RET_HEADS, RET_BLOCK, RET_DK))
    kdec = np.broadcast_to(kdec[:, :, None], (RET_HEADS, RET_BLOCK, RET_DK))
    cdec = np.broadcast_to(cdec[:, None, None], (RET_HEADS, 8, RET_DV))
    return tuple(jnp.asarray(a, F32) for a in (dmat, qdec, kdec, cdec))


def _chunkattn_kernel(q_ref, k_ref, v_ref, bias_ref, o_ref, kpad_ref, vt_ref, s_ref, p_ref, ot_ref):
    s_len = q_ref.shape[0]
    n_blocks = s_len // CA_QBLOCK
    n_pad = CA_PAD // CA_QBLOCK
    kpad_ref[0:CA_PAD, :] = jnp.zeros((CA_PAD, CA_W), BF16)
    kpad_ref[CA_PAD:, :] = k_ref[...]
    ones = jnp.ones((ONES_ROWS, CA_QBLOCK), BF16)
    for c in range(n_pad + n_blocks):
        if c < n_pad:
            vt = jnp.zeros((CA_W, CA_QBLOCK), BF16)
        else:
            vt = v_ref[(c - n_pad) * CA_QBLOCK:(c - n_pad + 1) * CA_QBLOCK, :].astype(F32).T.astype(BF16)
        for h in range(CA_HEADS):
            vt_ref[c, h * CA_SLAB:h * CA_SLAB + CA_DH, :] = vt[h * CA_DH:(h + 1) * CA_DH, :]
            vt_ref[c, h * CA_SLAB + CA_DH:(h + 1) * CA_SLAB, :] = ones
    p_ref[:, CA_HWIN:CA_WIN, 0:CA_HALF] = jnp.zeros((2, CA_WIN - CA_HWIN, CA_HALF), BF16)
    p_ref[:, 0:CA_HALF, CA_HALF:CA_QBLOCK] = jnp.zeros((2, CA_HALF, CA_HALF), BF16)
    lane = lax.broadcasted_iota(jnp.int32, (CA_QBLOCK, LANES), 1)
    head_lanes = (lane < CA_DH, lane >= CA_DH)
    win_row = lax.broadcasted_iota(jnp.int32, (CA_HWIN, CA_HALF), 0)

    def row0(i):
        return i * CA_QBLOCK if isinstance(i, int) else pl.multiple_of(i * CA_QBLOCK, CA_QBLOCK)

    def scores(i, h):
        pair = slice((h // 2) * LANES, (h // 2 + 1) * LANES)
        q = q_ref[pl.ds(row0(i), CA_QBLOCK), pair]
        qz = jnp.where(head_lanes[h % 2], q, jnp.zeros_like(q))
        s_ref[h % CA_DEPTH] = _dot_nt(kpad_ref[pl.ds(row0(i), CA_WIN), pair], qz)

    def attend(i, h, first_key):
        for half in range(2):
            rows = slice(half * CA_HALF, half * CA_HALF + CA_HWIN)
            cols = slice(half * CA_HALF, (half + 1) * CA_HALF)
            s = s_ref[h % CA_DEPTH, rows, cols] + bias_ref[h]
            if first_key is not None:
                s = jnp.where(win_row >= -(first_key + half * CA_HALF), s, NEG_INF)
            m = jnp.max(s, axis=0, keepdims=True)
            p_ref[h % 2, rows, cols] = jnp.exp2(s - m).astype(BF16)
        pv = None
        for c in range(CA_WIN // CA_QBLOCK):
            part = _dot(vt_ref[i + c, h * CA_SLAB:(h + 1) * CA_SLAB, :],
                        p_ref[h % 2, c * CA_QBLOCK:(c + 1) * CA_QBLOCK, :])
            pv = part if pv is None else pv + part
        ot_ref[h * CA_DH:(h + 1) * CA_DH, :] = pv[0:CA_DH] * (1.0 / pv[CA_DH:CA_DH + 1])

    def qblock(i, first_key, nxt):
        for h in range(CA_HEADS):
            ahead = h + CA_AHEAD
            if ahead < CA_HEADS:
                scores(i, ahead)
            elif nxt is not None:
                scores(nxt, ahead - CA_HEADS)
            attend(i, h, first_key)
        o_ref[pl.ds(row0(i), CA_QBLOCK), :] = ot_ref[...].T.astype(BF16)

    for h in range(CA_AHEAD):
        scores(0, h)
    for i in range(n_pad):
        qblock(i, i * CA_QBLOCK - CA_PAD, i + 1)

    def body(i, carry):
        qblock(i, None, i + 1)
        return carry

    lax.fori_loop(n_pad, n_blocks - 1, body, 0)
    qblock(n_blocks - 1, None, None)


def _chunkattn(h3, bias):
    b, s, _ = h3.shape
    n_blocks = s // CA_QBLOCK
    return pl.pallas_call(
        _chunkattn_kernel,
        grid=(b,),
        in_specs=[pl.BlockSpec((None, s, CA_W), lambda i: (i, 0, OFF_CQ // CA_W)),
                  pl.BlockSpec((None, s, CA_W), lambda i: (i, 0, OFF_CK // CA_W)),
                  pl.BlockSpec((None, s, CA_W), lambda i: (i, 0, OFF_CV // CA_W)),
                  _const_spec((CA_HEADS, CA_HWIN, CA_HALF))],
        out_specs=pl.BlockSpec((None, s, CA_W), lambda i: (i, 0, 0)),
        out_shape=jax.ShapeDtypeStruct((b, s, CA_W), BF16),
        scratch_shapes=[pltpu.VMEM((s + CA_PAD, CA_W), BF16),
                        pltpu.VMEM((n_blocks + CA_PAD // CA_QBLOCK, CA_HEADS * CA_SLAB, CA_QBLOCK), BF16),
                        pltpu.VMEM((CA_DEPTH, CA_WIN, CA_QBLOCK), F32),
                        pltpu.VMEM((2, CA_WIN, CA_QBLOCK), BF16),
                        pltpu.VMEM((CA_W, CA_QBLOCK), F32)],
        compiler_params=_params(1),
        name="chunkattn",
    )(h3, h3, h3, bias)


def _chunkattn_bias(rel_bias):
    j = np.arange(CA_HWIN)[:, None]
    r = np.arange(CA_HALF)[None, :]
    qc, kc = r // CHUNK, j // CHUNK
    band = (kc >= qc) & (kc <= qc + CA_LEFT_CHUNKS)
    n_diag = CA_HWIN + CA_HALF
    u = np.arange(n_diag)
    d = np.where(u < CA_HALF, u, u - n_diag)
    idx = np.clip(d + CA_PAD, -REL_CLIP, REL_CLIP) + REL_CLIP
    diag = rel_bias[:, idx].astype(F32) * LOG2E
    shifted = jnp.tile(diag, (1, CA_HWIN))[:, :CA_HWIN * (n_diag - 1)].reshape(CA_HEADS, CA_HWIN, n_diag - 1)
    table = shifted[:, :, :CA_HALF]
    return jnp.where(jnp.asarray(band)[None], table, NEG_INF)


def _diffattn_kernel(q_ref, k_ref, v_ref, lq1_ref, lk1_ref, lq2_ref, lk2_ref, linit_ref, ng_ref, o_ref,
                     vt_ref, s_ref, pv_ref):
    s_len = q_ref.shape[0]
    n_blocks = s_len // DA_BLOCK
    hw = 2 * DA_DH
    lam = (jnp.exp(jnp.sum(lq1_ref[...] * lk1_ref[...], axis=-1, keepdims=True))
           - jnp.exp(jnp.sum(lq2_ref[...] * lk2_ref[...], axis=-1, keepdims=True)) + linit_ref[...])
    out_scale = ng_ref[...] * (1.0 - linit_ref[...])
    for c in range(n_blocks):
        blk = slice(c * DA_BLOCK, (c + 1) * DA_BLOCK)
        vt = v_ref[blk, :].astype(F32).T.astype(BF16)
        for pr in range(DA_PAIRS):
            vt_ref[pr, 0:hw, blk] = vt[pr * hw:(pr + 1) * hw, :]
    vt_ref[:, hw:, :] = jnp.ones((DA_PAIRS, ONES_ROWS, s_len), BF16)
    key_chunk = lax.broadcasted_iota(jnp.int32, (DA_BLOCK, DA_BLOCK), 0) // CHUNK
    query_chunk = lax.broadcasted_iota(jnp.int32, (DA_BLOCK, DA_BLOCK), 1) // CHUNK
    diag_allowed = key_chunk <= query_chunk
    lane = lax.broadcasted_iota(jnp.int32, (DA_BLOCK, hw), 1)
    head_lanes = (lane < DA_DH, lane >= DA_DH)

    steps = [(i, t, c, pr) for i in range(n_blocks) for t in range(2) for c in range(i + 1) for pr in range(DA_PAIRS)]
    maxima = {}

    def scores(n):
        i, t, c, pr = steps[n]
        lanes = slice(pr * hw, (pr + 1) * hw)
        q = q_ref[i * DA_BLOCK:(i + 1) * DA_BLOCK, lanes]
        qz = jnp.where(head_lanes[t], q, jnp.zeros_like(q))
        s = _dot_nt(k_ref[c * DA_BLOCK:(c + 1) * DA_BLOCK, lanes], qz)
        if c == i:
            s = jnp.where(diag_allowed, s, NEG_INF)
        s_ref[n % DA_RING] = s
        maxima[n] = jnp.max(s, axis=0, keepdims=True)

    def attend(n):
        i, t, c, pr = steps[n]
        p = jnp.exp2(s_ref[n % DA_RING] - maxima[n]).astype(BF16)
        pv_ref[pr, i % 2, t, c] = _dot(vt_ref[pr, :, c * DA_BLOCK:(c + 1) * DA_BLOCK], p)
        if c == i:
            mine = [n - (i - j) * DA_PAIRS for j in range(i + 1)]
            m = functools.reduce(jnp.maximum, [maxima[j] for j in mine])
            acc = None
            for j, nj in enumerate(mine):
                part = pv_ref[pr, i % 2, t, j] * jnp.exp2(maxima.pop(nj) - m)
                acc = part if acc is None else acc + part
            heads[pr][t] = acc[0:hw] * (1.0 / acc[hw:hw + 1])
            if t == 1:
                o_t = heads[pr][0] - lam * heads[pr][1]
                ms = jnp.mean(o_t * o_t, axis=0, keepdims=True)
                y = (o_t * lax.rsqrt(ms + LN_EPS)).T * out_scale
                o_ref[i * DA_BLOCK:(i + 1) * DA_BLOCK, pr * hw:(pr + 1) * hw] = y.astype(BF16)

    heads = [[None, None] for _ in range(DA_PAIRS)]
    for n in range(len(steps) + DA_AHEAD):
        if n < len(steps):
            scores(n)
        if n >= DA_AHEAD:
            attend(n - DA_AHEAD)


def _diffattn(h3, lams, norm_g, lambda_init):
    b, s, _ = h3.shape
    hw = 2 * DA_DH
    bw = DA_PAIRS * hw
    small = pl.BlockSpec((1, DA_DH), lambda i, h: (0, 0))
    return pl.pallas_call(
        _diffattn_kernel,
        grid=(b, DA_HEADS // DA_PAIRS),
        in_specs=[pl.BlockSpec((None, s, bw), lambda i, h: (i, 0, OFF_DQ // bw + h)),
                  pl.BlockSpec((None, s, bw), lambda i, h: (i, 0, OFF_DK // bw + h)),
                  pl.BlockSpec((None, s, bw), lambda i, h: (i, 0, OFF_DV // bw + h)),
                  small, small, small, small,
                  pl.BlockSpec((1, 1), lambda i, h: (0, 0)),
                  pl.BlockSpec((1, hw), lambda i, h: (0, 0))],
        out_specs=pl.BlockSpec((None, s, bw), lambda i, h: (i, 0, h)),
        out_shape=jax.ShapeDtypeStruct((b, s, DA_V_W), BF16),
        scratch_shapes=[pltpu.VMEM((DA_PAIRS, hw + ONES_ROWS, s), BF16),
                        pltpu.VMEM((DA_RING, DA_BLOCK, DA_BLOCK), F32),
                        pltpu.VMEM((DA_PAIRS, 2, 2, s // DA_BLOCK, hw + ONES_ROWS, DA_BLOCK), F32)],
        compiler_params=_params(2),
        name="diffattn",
    )(h3, h3, h3, *lams, jnp.full((1, 1), lambda_init, F32), norm_g)


def _layer_norm(y, g, b):
    mu = jnp.mean(y, axis=-1, keepdims=True)
    d = y - mu
    var = jnp.mean(d * d, axis=-1, keepdims=True)
    return d * lax.rsqrt(var + LN_EPS) * g + b


def _merge_kernel(layer_ref, x_ref, oa_ref, ob_ref, oc_ref, wg_ref, bg_ref, wa_ref, wb_ref, wc_ref, wo_ref, g_ref, b_ref,
                  o_ref):
    for r0 in range(0, x_ref.shape[0], SUB_ROWS):
        rows = slice(r0, r0 + SUB_ROWS)
        xf = x_ref[rows, :]
        xb = xf.astype(BF16)
        merged = None
        for n, (br_ref, w_ref) in enumerate(((oa_ref, wa_ref), (ob_ref, wb_ref), (oc_ref, wc_ref))):
            cols = slice(n * D_MODEL, (n + 1) * D_MODEL)
            z = _dot(xb, wg_ref[:, cols]) + bg_ref[:, cols]
            term = _dot(br_ref[rows, :], w_ref[...]) * (1.0 / (1.0 + jnp.exp(-z)))
            merged = term if merged is None else merged + term
        mix = _dot(merged.astype(BF16), wo_ref[...])
        o_ref[rows, :] = _layer_norm(DEEPNORM_ALPHA * xf + mix, g_ref[...], b_ref[...])


def _merge(layer, x2, oa, ob, oc, w_in, bg, wa, wb, wc, wo, g, b, tm):
    t = x2.shape[0]
    row = lambda w: pl.BlockSpec((tm, w), lambda i, l: (i, 0))
    return pl.pallas_call(
        _merge_kernel,
        grid_spec=pltpu.PrefetchScalarGridSpec(
            num_scalar_prefetch=1,
            grid=(t // tm,),
            in_specs=[row(D_MODEL), row(RET_V_W), row(CA_W), row(DA_V_W),
                      pl.BlockSpec((None, D_MODEL, GATE_W), lambda i, l: (l[0], 0, MIX_W // GATE_W),
                                   pipeline_mode=pl.Buffered(1)),
                      _const_spec((1, GATE_W)),
                      _const_spec((RET_V_W, D_MODEL)), _const_spec((CA_W, D_MODEL)), _const_spec((DA_V_W, D_MODEL)),
                      _const_spec((D_MODEL, D_MODEL)), _const_spec((1, D_MODEL)), _const_spec((1, D_MODEL))],
            out_specs=row(D_MODEL)),
        out_shape=jax.ShapeDtypeStruct((t, D_MODEL), F32),
        compiler_params=_params(1),
        name="merge",
    )(layer, x2, oa, ob, oc, w_in, bg, wa, wb, wc, wo, g, b)


FFN_CHUNK = 256


def _ffn_kernel(x_ref, wi_ref, wo_ref, g_ref, b_ref, o_ref, act_ref):
    for r0 in range(0, x_ref.shape[0], SUB_ROWS):
        rows = slice(r0, r0 + SUB_ROWS)
        xf = x_ref[rows, :]
        xb = xf.astype(BF16)
        for c in range(0, FFN_HIDDEN, FFN_CHUNK):
            ug = _dot(xb, wi_ref[:, c:c + FFN_CHUNK])
            uu = _dot(xb, wi_ref[:, FFN_HIDDEN + c:FFN_HIDDEN + c + FFN_CHUNK])
            act_ref[rows, c:c + FFN_CHUNK] = (ug / (1.0 + jnp.exp(-ug)) * uu).astype(BF16)
        ffn = _dot(act_ref[rows, :], wo_ref[...])
        o_ref[rows, :] = _layer_norm(DEEPNORM_ALPHA * xf + ffn, g_ref[...], b_ref[...])


def _ffn(x2, wi, wo, g, b, tm):
    t = x2.shape[0]
    return pl.pallas_call(
        _ffn_kernel,
        grid=(t // tm,),
        in_specs=[pl.BlockSpec((tm, D_MODEL), lambda i: (i, 0)),
                  _const_spec((D_MODEL, 2 * FFN_HIDDEN)), _const_spec((FFN_HIDDEN, D_MODEL)),
                  _const_spec((1, D_MODEL)), _const_spec((1, D_MODEL))],
        out_specs=pl.BlockSpec((tm, D_MODEL), lambda i: (i, 0)),
        out_shape=jax.ShapeDtypeStruct((t, D_MODEL), F32),
        scratch_shapes=[pltpu.VMEM((tm, FFN_HIDDEN), BF16)],
        compiler_params=_params(1),
        name="ffn",
    )(x2, wi, wo, g, b)


def _rotary_tables(seq):
    pos = jnp.arange(seq, dtype=F32)[:, None]

    def cs(d):
        inv_freq = ROPE_THETA ** (-jnp.arange(0, d, 2, dtype=F32) / d)
        ang = pos * inv_freq[None, :]
        return jnp.cos(ang), jnp.sin(ang)

    rc, rs = cs(RET_DK)
    rcos = jnp.concatenate([rc, rc], axis=1)
    rsin = jnp.concatenate([-rs, rs], axis=1)
    dc, ds = cs(DA_DH)
    z = jnp.zeros_like(ds)
    dcos = jnp.concatenate([dc, dc, dc, dc], axis=1)
    dsina = jnp.concatenate([-ds, z, -ds, z], axis=1)
    dsinb = jnp.concatenate([z, ds, z, ds], axis=1)
    return rcos, rsin, dcos, dsina, dsinb


def kernel(x, w_in, ret_norm_g, ca_rel_bias, da_lambda_q1, da_lambda_k1, da_lambda_q2, da_lambda_k2, da_norm_g, w_branch_a, w_branch_b, w_branch_c, b_merge, w_out, ln1_g, ln1_b, w_ffn_in, w_ffn_out, ln2_g, ln2_b):
    b, s, d = x.shape
    t = b * s
    rot_tabs = _rotary_tables(s)
    ret_tabs = _retention_tables()
    row = lambda a: a.reshape(1, -1).astype(F32)

    assert MIX_W % GATE_W == 0
    w_in_b = w_in.astype(BF16)
    x2 = x.reshape(t, d)
    for l in range(DEPTH):
        lambda_init = 0.8 - 0.6 * math.exp(-0.3 * l)
        layer = jnp.full((1,), l, jnp.int32)
        h = _inproj(layer, x2, w_in_b, rot_tabs, s, PROJ_ROWS)
        h3 = h.reshape(b, s, MIX_W)
        o_a = _retention(h3, ret_tabs, row(ret_norm_g[l]))
        o_b = _chunkattn(h3, _chunkattn_bias(ca_rel_bias[l]))
        o_c = _diffattn(h3, (row(da_lambda_q1[l]), row(da_lambda_k1[l]), row(da_lambda_q2[l]), row(da_lambda_k2[l])),
                        row(da_norm_g[l]), lambda_init)
        x2 = _merge(layer, x2, o_a.reshape(t, RET_V_W), o_b.reshape(t, CA_W), o_c.reshape(t, DA_V_W),
                    w_in_b, row(b_merge[l]), w_branch_a[l].astype(BF16), w_branch_b[l].astype(BF16),
                    w_branch_c[l].astype(BF16), w_out[l].astype(BF16), row(ln1_g[l]), row(ln1_b[l]), TAIL_ROWS)
        x2 = _ffn(x2, w_ffn_in[l].astype(BF16), w_ffn_out[l].astype(BF16), row(ln2_g[l]), row(ln2_b[l]), TAIL_ROWS)
    return x2.reshape(b, s, d)
```

```python
import functools
import math

import jax
import jax.numpy as jnp
import numpy as np
from jax import lax
from jax.experimental import pallas as pl
from jax.experimental.pallas import tpu as pltpu

D_MODEL = 1024
DEPTH = 4
CHUNK = 64
RET_HEADS, RET_DK, RET_DV = 4, 128, 256
CA_HEADS, CA_DH, CA_LEFT_CHUNKS, REL_CLIP = 8, 64, 8, 256
DA_HEADS, DA_DH = 4, 64
FFN_HIDDEN = -(-8 * D_MODEL // (3 * 256)) * 256
ROPE_THETA = 10000.0
LN_EPS = 1e-5
NEG_INF = -1e30
DEEPNORM_ALPHA = (2.0 * DEPTH) ** 0.25

RET_QK_W = RET_HEADS * RET_DK
RET_V_W = RET_HEADS * RET_DV
CA_W = CA_HEADS * CA_DH
DA_QK_W = 2 * DA_HEADS * DA_DH
DA_V_W = DA_HEADS * 2 * DA_DH
OFF_RQ = 0
OFF_RK = OFF_RQ + RET_QK_W
OFF_RV = OFF_RK + RET_QK_W
OFF_RG = OFF_RV + RET_V_W
OFF_CQ = OFF_RG + RET_V_W
OFF_CK = OFF_CQ + CA_W
OFF_CV = OFF_CK + CA_W
OFF_DQ = OFF_CV + CA_W
OFF_DK = OFF_DQ + DA_QK_W
OFF_DV = OFF_DK + DA_QK_W
MIX_W = OFF_DV + DA_V_W
GATE_W = 3 * D_MODEL

LOG2E = math.log2(math.e)
LANES = 128
ONES_ROWS = 16
VMEM_LIMIT = 56 * 1024 * 1024

RET_BLOCK = 256
CA_QBLOCK = 256
CA_HALF = CA_QBLOCK // 2
CA_PAD = CA_LEFT_CHUNKS * CHUNK
CA_WIN = CA_PAD + CA_QBLOCK
CA_HWIN = CA_PAD + CA_HALF
CA_AHEAD = 5
CA_DEPTH = CA_AHEAD + 1
CA_SLAB = CA_DH + ONES_ROWS
DA_BLOCK = 256
DA_PAIRS = 2
DA_AHEAD = 6 * DA_PAIRS
DA_RING = DA_AHEAD + 1
SUB_ROWS = 256
PROJ_ROWS = 512
TAIL_ROWS = 4 * SUB_ROWS

BF16 = jnp.bfloat16
F32 = jnp.float32


def _dot(a, b):
    return jnp.dot(a, b, preferred_element_type=F32)


def _dot_nt(a, b):
    return lax.dot_general(a, b, (((1,), (1,)), ((), ())), preferred_element_type=F32)


def _dot_tn(a, b):
    return lax.dot_general(a, b, (((0,), (0,)), ((), ())), preferred_element_type=F32)


def _const_spec(shape):
    zeros = (0,) * len(shape)
    return pl.BlockSpec(shape, lambda *_: zeros, pipeline_mode=pl.Buffered(1))


def _params(n_axes):
    return pltpu.CompilerParams(dimension_semantics=("arbitrary",) * n_axes, vmem_limit_bytes=VMEM_LIMIT)


def _inproj_kernel(layer_ref, x_ref, w_ref, rcos_ref, rsin_ref, dcos_ref, dsina_ref, dsinb_ref, o_ref):
    xb = x_ref[...].astype(BF16)

    def proj(off, width):
        return _dot(xb, w_ref[:, off:off + width])

    def ret_rotary(a, scale):
        outs = []
        for h in range(RET_HEADS):
            ah = a[:, h * RET_DK:(h + 1) * RET_DK]
            r = ah * rcos_ref[...] + pltpu.roll(ah, RET_DK // 2, 1) * rsin_ref[...]
            outs.append(r * scale if scale != 1.0 else r)
        return jnp.concatenate(outs, axis=1)

    def da_rotary(a, scale):
        outs = []
        for g in range(DA_QK_W // LANES):
            ag = a[:, g * LANES:(g + 1) * LANES]
            r = (ag * dcos_ref[...] + pltpu.roll(ag, LANES - DA_DH // 2, 1) * dsina_ref[...]
                 + pltpu.roll(ag, DA_DH // 2, 1) * dsinb_ref[...])
            outs.append(r * scale if scale != 1.0 else r)
        return jnp.concatenate(outs, axis=1)

    o_ref[:, OFF_RQ:OFF_RQ + RET_QK_W] = ret_rotary(proj(OFF_RQ, RET_QK_W), 1.0).astype(BF16)
    o_ref[:, OFF_RK:OFF_RK + RET_QK_W] = ret_rotary(proj(OFF_RK, RET_QK_W), RET_DK ** -0.5).astype(BF16)
    for off in range(OFF_RV, OFF_CQ, 512):
        o_ref[:, off:off + 512] = proj(off, 512).astype(BF16)
    o_ref[:, OFF_CQ:OFF_CQ + CA_W] = (proj(OFF_CQ, CA_W) * (CA_DH ** -0.5 * LOG2E)).astype(BF16)
    o_ref[:, OFF_CK:OFF_CK + CA_W] = proj(OFF_CK, CA_W).astype(BF16)
    o_ref[:, OFF_CV:OFF_CV + CA_W] = proj(OFF_CV, CA_W).astype(BF16)
    o_ref[:, OFF_DQ:OFF_DQ + DA_QK_W] = da_rotary(proj(OFF_DQ, DA_QK_W), DA_DH ** -0.5 * LOG2E).astype(BF16)
    o_ref[:, OFF_DK:OFF_DK + DA_QK_W] = da_rotary(proj(OFF_DK, DA_QK_W), 1.0).astype(BF16)
    o_ref[:, OFF_DV:OFF_DV + DA_V_W] = proj(OFF_DV, DA_V_W).astype(BF16)


def _inproj(layer, x2, w_in, tabs, seq, tm):
    t = x2.shape[0]
    pos_blocks = seq // tm
    tab_spec = pl.BlockSpec((tm, LANES), lambda i, l: (i % pos_blocks, 0))
    return pl.pallas_call(
        _inproj_kernel,
        grid_spec=pltpu.PrefetchScalarGridSpec(
            num_scalar_prefetch=1,
            grid=(t // tm,),
            in_specs=[pl.BlockSpec((tm, D_MODEL), lambda i, l: (i, 0)),
                      pl.BlockSpec((None, D_MODEL, MIX_W), lambda i, l: (l[0], 0, 0), pipeline_mode=pl.Buffered(1)),
                      tab_spec, tab_spec, tab_spec, tab_spec, tab_spec],
            out_specs=pl.BlockSpec((tm, MIX_W), lambda i, l: (i, 0))),
        out_shape=jax.ShapeDtypeStruct((t, MIX_W), BF16),
        compiler_params=_params(1),
        name="inproj",
    )(layer, x2, w_in, *tabs)


def _retention_kernel(q_ref, k_ref, v_ref, g_ref, dmat_ref, qdec_ref, kdec_ref, cdec_ref, ng_ref, o_ref,
                      state_ref, sv_ref, kv_ref):
    n_blocks = q_ref.shape[0] // RET_BLOCK

    def local(h, c):
        rows = slice(c * RET_BLOCK, (c + 1) * RET_BLOCK)
        q = q_ref[rows, h * RET_DK:(h + 1) * RET_DK]
        k = k_ref[rows, h * RET_DK:(h + 1) * RET_DK]
        v = v_ref[rows, h * RET_DV:(h + 1) * RET_DV]
        s = _dot_nt(q, k) * dmat_ref[h]
        sv_ref[c % 2, h] = _dot(s.astype(BF16), v)
        kd = (k.astype(F32) * kdec_ref[h]).astype(BF16)
        kv_ref[c % 2, h] = _dot_tn(kd, v)

    def finish(h, c):
        rows = slice(c * RET_BLOCK, (c + 1) * RET_BLOCK)
        cols = slice(h * RET_DV, (h + 1) * RET_DV)
        o = sv_ref[c % 2, h]
        if c > 0:
            qd = (q_ref[rows, h * RET_DK:(h + 1) * RET_DK].astype(F32) * qdec_ref[h]).astype(BF16)
            o = o + _dot(qd, state_ref[h].astype(BF16))
            state_ref[h] = state_ref[h] * cdec_ref[h, 0:1, :] + kv_ref[c % 2, h]
        else:
            state_ref[h] = kv_ref[c % 2, h]
        ms = jnp.mean(o * o, axis=-1, keepdims=True)
        y = o * lax.rsqrt(ms + LN_EPS) * ng_ref[...]
        gate = g_ref[rows, cols].astype(F32)
        o_ref[rows, cols] = (gate / (1.0 + jnp.exp(-gate)) * y).astype(BF16)

    for h in range(RET_HEADS):
        local(h, 0)
    for c in range(n_blocks):
        for h in range(RET_HEADS):
            if c + 1 < n_blocks:
                local(h, c + 1)
            finish(h, c)


def _retention(h3, tabs, norm_g):
    b, s, _ = h3.shape
    dmat, qdec, kdec, cdec = tabs
    return pl.pallas_call(
        _retention_kernel,
        grid=(b,),
        in_specs=[pl.BlockSpec((None, s, RET_QK_W), lambda i: (i, 0, OFF_RQ // RET_QK_W)),
                  pl.BlockSpec((None, s, RET_QK_W), lambda i: (i, 0, OFF_RK // RET_QK_W)),
                  pl.BlockSpec((None, s, RET_V_W), lambda i: (i, 0, OFF_RV // RET_V_W)),
                  pl.BlockSpec((None, s, RET_V_W), lambda i: (i, 0, OFF_RG // RET_V_W)),
                  _const_spec((RET_HEADS, RET_BLOCK, RET_BLOCK)),
                  _const_spec((RET_HEADS, RET_BLOCK, RET_DK)),
                  _const_spec((RET_HEADS, RET_BLOCK, RET_DK)),
                  _const_spec((RET_HEADS, 8, RET_DV)),
                  _const_spec((1, RET_DV))],
        out_specs=pl.BlockSpec((None, s, RET_V_W), lambda i: (i, 0, 0)),
        out_shape=jax.ShapeDtypeStruct((b, s, RET_V_W), BF16),
        scratch_shapes=[pltpu.VMEM((RET_HEADS, RET_DK, RET_DV), F32),
                        pltpu.VMEM((2, RET_HEADS, RET_BLOCK, RET_DV), F32),
                        pltpu.VMEM((2, RET_HEADS, RET_DK, RET_DV), F32)],
        compiler_params=_params(1),
        name="retention",
    )(h3, h3, h3, h3, dmat, qdec, kdec, cdec, norm_g)


def _retention_tables():
    h = np.arange(RET_HEADS, dtype=np.float64)
    log_gamma = np.log1p(-np.exp2(-5.0 - h))
    n = np.arange(RET_BLOCK)
    diff = (n[:, None] - n[None, :]).astype(np.float64)
    cn, cm = n[:, None] // CHUNK, n[None, :] // CHUNK
    expo = np.where(cm == cn, np.abs(diff), diff)
    dmat = np.where(cm <= cn, np.exp(log_gamma[:, None, None] * expo[None]), 0.0)
    qdec = np.exp(log_gamma[:, None] * (n[None, :] + 1.0))
    kdec = np.exp(log_gamma[:, None] * (RET_BLOCK - 1.0 - n[None, :]))
    cdec = np.exp(log_gamma * RET_BLOCK)
    qdec = np.broadcast_to(qdec[:, :, None], (RET_HEADS, RET_BLOCK, RET_DK))
    kdec = np.broadcast_to(kdec[:, :, None], (RET_HEADS, RET_BLOCK, RET_DK))
    cdec = np.broadcast_to(cdec[:, None, None], (RET_HEADS, 8, RET_DV))
    return tuple(jnp.asarray(a, F32) for a in (dmat, qdec, kdec, cdec))


def _chunkattn_kernel(q_ref, k_ref, v_ref, bias_ref, o_ref, kpad_ref, vt_ref, s_ref, p_ref, ot_ref):
    s_len = q_ref.shape[0]
    n_blocks = s_len // CA_QBLOCK
    n_pad = CA_PAD // CA_QBLOCK
    kpad_ref[0:CA_PAD, :] = jnp.zeros((CA_PAD, CA_W), BF16)
    kpad_ref[CA_PAD:, :] = k_ref[...]
    ones = jnp.ones((ONES_ROWS, CA_QBLOCK), BF16)
    for c in range(n_pad + n_blocks):
        if c < n_pad:
            vt = jnp.zeros((CA_W, CA_QBLOCK), BF16)
        else:
            vt = v_ref[(c - n_pad) * CA_QBLOCK:(c - n_pad + 1) * CA_QBLOCK, :].astype(F32).T.astype(BF16)
        for h in range(CA_HEADS):
            vt_ref[c, h * CA_SLAB:h * CA_SLAB + CA_DH, :] = vt[h * CA_DH:(h + 1) * CA_DH, :]
            vt_ref[c, h * CA_SLAB + CA_DH:(h + 1) * CA_SLAB, :] = ones
    p_ref[:, CA_HWIN:CA_WIN, 0:CA_HALF] = jnp.zeros((2, CA_WIN - CA_HWIN, CA_HALF), BF16)
    p_ref[:, 0:CA_HALF, CA_HALF:CA_QBLOCK] = jnp.zeros((2, CA_HALF, CA_HALF), BF16)
    lane = lax.broadcasted_iota(jnp.int32, (CA_QBLOCK, LANES), 1)
    head_lanes = (lane < CA_DH, lane >= CA_DH)
    win_row = lax.broadcasted_iota(jnp.int32, (CA_HWIN, CA_HALF), 0)

    def scores(n):
        i, h = divmod(n, CA_HEADS)
        pair = slice((h // 2) * LANES, (h // 2 + 1) * LANES)
        q = q_ref[i * CA_QBLOCK:(i + 1) * CA_QBLOCK, pair]
        qz = jnp.where(head_lanes[h % 2], q, jnp.zeros_like(q))
        s_ref[n % CA_DEPTH] = _dot_nt(kpad_ref[i * CA_QBLOCK:i * CA_QBLOCK + CA_WIN, pair], qz)

    def attend(n):
        i, h = divmod(n, CA_HEADS)
        first_key = i * CA_QBLOCK - CA_PAD
        for half in range(2):
            rows = slice(half * CA_HALF, half * CA_HALF + CA_HWIN)
            cols = slice(half * CA_HALF, (half + 1) * CA_HALF)
            s = s_ref[n % CA_DEPTH, rows, cols] + bias_ref[h]
            if first_key + half * CA_HALF < 0:
                s = jnp.where(win_row >= -(first_key + half * CA_HALF), s, NEG_INF)
            m = jnp.max(s, axis=0, keepdims=True)
            p_ref[h % 2, rows, cols] = jnp.exp2(s - m).astype(BF16)
        pv = None
        for c in range(CA_WIN // CA_QBLOCK):
            part = _dot(vt_ref[i + c, h * CA_SLAB:(h + 1) * CA_SLAB, :],
                        p_ref[h % 2, c * CA_QBLOCK:(c + 1) * CA_QBLOCK, :])
            pv = part if pv is None else pv + part
        ot_ref[h * CA_DH:(h + 1) * CA_DH, :] = pv[0:CA_DH] * (1.0 / pv[CA_DH:CA_DH + 1])
        if h == CA_HEADS - 1:
            o_ref[i * CA_QBLOCK:(i + 1) * CA_QBLOCK, :] = ot_ref[...].T.astype(BF16)

    n_steps = n_blocks * CA_HEADS
    for n in range(n_steps + CA_AHEAD):
        if n < n_steps:
            scores(n)
        if n >= CA_AHEAD:
            attend(n - CA_AHEAD)


def _chunkattn(h3, bias):
    b, s, _ = h3.shape
    n_blocks = s // CA_QBLOCK
    return pl.pallas_call(
        _chunkattn_kernel,
        grid=(b,),
        in_specs=[pl.BlockSpec((None, s, CA_W), lambda i: (i, 0, OFF_CQ // CA_W)),
                  pl.BlockSpec((None, s, CA_W), lambda i: (i, 0, OFF_CK // CA_W)),
                  pl.BlockSpec((None, s, CA_W), lambda i: (i, 0, OFF_CV // CA_W)),
                  _const_spec((CA_HEADS, CA_HWIN, CA_HALF))],
        out_specs=pl.BlockSpec((None, s, CA_W), lambda i: (i, 0, 0)),
        out_shape=jax.ShapeDtypeStruct((b, s, CA_W), BF16),
        scratch_shapes=[pltpu.VMEM((s + CA_PAD, CA_W), BF16),
                        pltpu.VMEM((n_blocks + CA_PAD // CA_QBLOCK, CA_HEADS * CA_SLAB, CA_QBLOCK), BF16),
                        pltpu.VMEM((CA_DEPTH, CA_WIN, CA_QBLOCK), F32),
                        pltpu.VMEM((2, CA_WIN, CA_QBLOCK), BF16),
                        pltpu.VMEM((CA_W, CA_QBLOCK), F32)],
        compiler_params=_params(1),
        name="chunkattn",
    )(h3, h3, h3, bias)


def _chunkattn_bias(rel_bias):
    j = np.arange(CA_HWIN)[:, None]
    r = np.arange(CA_HALF)[None, :]
    qc, kc = r // CHUNK, j // CHUNK
    band = (kc >= qc) & (kc <= qc + CA_LEFT_CHUNKS)
    n_diag = CA_HWIN + CA_HALF
    u = np.arange(n_diag)
    d = np.where(u < CA_HALF, u, u - n_diag)
    idx = np.clip(d + CA_PAD, -REL_CLIP, REL_CLIP) + REL_CLIP
    diag = rel_bias[:, idx].astype(F32) * LOG2E
    shifted = jnp.tile(diag, (1, CA_HWIN))[:, :CA_HWIN * (n_diag - 1)].reshape(CA_HEADS, CA_HWIN, n_diag - 1)
    table = shifted[:, :, :CA_HALF]
    return jnp.where(jnp.asarray(band)[None], table, NEG_INF)


def _diffattn_kernel(q_ref, k_ref, v_ref, lq1_ref, lk1_ref, lq2_ref, lk2_ref, linit_ref, ng_ref, o_ref,
                     vt_ref, s_ref, pv_ref):
    s_len = q_ref.shape[0]
    n_blocks = s_len // DA_BLOCK
    hw = 2 * DA_DH
    lam = (jnp.exp(jnp.sum(lq1_ref[...] * lk1_ref[...], axis=-1, keepdims=True))
           - jnp.exp(jnp.sum(lq2_ref[...] * lk2_ref[...], axis=-1, keepdims=True)) + linit_ref[...])
    out_scale = ng_ref[...] * (1.0 - linit_ref[...])
    for c in range(n_blocks):
        blk = slice(c * DA_BLOCK, (c + 1) * DA_BLOCK)
        vt = v_ref[blk, :].astype(F32).T.astype(BF16)
        for pr in range(DA_PAIRS):
            vt_ref[pr, 0:hw, blk] = vt[pr * hw:(pr + 1) * hw, :]
    vt_ref[:, hw:, :] = jnp.ones((DA_PAIRS, ONES_ROWS, s_len), BF16)
    key_chunk = lax.broadcasted_iota(jnp.int32, (DA_BLOCK, DA_BLOCK), 0) // CHUNK
    query_chunk = lax.broadcasted_iota(jnp.int32, (DA_BLOCK, DA_BLOCK), 1) // CHUNK
    diag_allowed = key_chunk <= query_chunk
    lane = lax.broadcasted_iota(jnp.int32, (DA_BLOCK, hw), 1)
    head_lanes = (lane < DA_DH, lane >= DA_DH)

    steps = [(i, t, c, pr) for i in range(n_blocks) for t in range(2) for c in range(i + 1) for pr in range(DA_PAIRS)]
    maxima = {}

    def scores(n):
        i, t, c, pr = steps[n]
        lanes = slice(pr * hw, (pr + 1) * hw)
        q = q_ref[i * DA_BLOCK:(i + 1) * DA_BLOCK, lanes]
        qz = jnp.where(head_lanes[t], q, jnp.zeros_like(q))
        s = _dot_nt(k_ref[c * DA_BLOCK:(c + 1) * DA_BLOCK, lanes], qz)
        if c == i:
            s = jnp.where(diag_allowed, s, NEG_INF)
        s_ref[n % DA_RING] = s
        maxima[n] = jnp.max(s, axis=0, keepdims=True)

    def attend(n):
        i, t, c, pr = steps[n]
        p = jnp.exp2(s_ref[n % DA_RING] - maxima[n]).astype(BF16)
        pv_ref[pr, i % 2, t, c] = _dot(vt_ref[pr, :, c * DA_BLOCK:(c + 1) * DA_BLOCK], p)
        if c == i:
            mine = [n - (i - j) * DA_PAIRS for j in range(i + 1)]
            m = functools.reduce(jnp.maximum, [maxima[j] for j in mine])
            acc = None
            for j, nj in enumerate(mine):
                part = pv_ref[pr, i % 2, t, j] * jnp.exp2(maxima.pop(nj) - m)
                acc = part if acc is None else acc + part
            heads[pr][t] = acc[0:hw] * (1.0 / acc[hw:hw + 1])
            if t == 1:
                o_t = heads[pr][0] - lam * heads[pr][1]
                ms = jnp.mean(o_t * o_t, axis=0, keepdims=True)
                y = (o_t * lax.rsqrt(ms + LN_EPS)).T * out_scale
                o_ref[i * DA_BLOCK:(i + 1) * DA_BLOCK, pr * hw:(pr + 1) * hw] = y.astype(BF16)

    heads = [[None, None] for _ in range(DA_PAIRS)]
    for n in range(len(steps) + DA_AHEAD):
        if n < len(steps):
            scores(n)
        if n >= DA_AHEAD:
            attend(n - DA_AHEAD)


def _diffattn(h3, lams, norm_g, lambda_init):
    b, s, _ = h3.shape
    hw = 2 * DA_DH
    bw = DA_PAIRS * hw
    small = pl.BlockSpec((1, DA_DH), lambda i, h: (0, 0))
    return pl.pallas_call(
        _diffattn_kernel,
        grid=(b, DA_HEADS // DA_PAIRS),
        in_specs=[pl.BlockSpec((None, s, bw), lambda i, h: (i, 0, OFF_DQ // bw + h)),
                  pl.BlockSpec((None, s, bw), lambda i, h: (i, 0, OFF_DK // bw + h)),
                  pl.BlockSpec((None, s, bw), lambda i, h: (i, 0, OFF_DV // bw + h)),
                  small, small, small, small,
                  pl.BlockSpec((1, 1), lambda i, h: (0, 0)),
                  pl.BlockSpec((1, hw), lambda i, h: (0, 0))],
        out_specs=pl.BlockSpec((None, s, bw), lambda i, h: (i, 0, h)),
        out_shape=jax.ShapeDtypeStruct((b, s, DA_V_W), BF16),
        scratch_shapes=[pltpu.VMEM((DA_PAIRS, hw + ONES_ROWS, s), BF16),
                        pltpu.VMEM((DA_RING, DA_BLOCK, DA_BLOCK), F32),
                        pltpu.VMEM((DA_PAIRS, 2, 2, s // DA_BLOCK, hw + ONES_ROWS, DA_BLOCK), F32)],
        compiler_params=_params(2),
        name="diffattn",
    )(h3, h3, h3, *lams, jnp.full((1, 1), lambda_init, F32), norm_g)


def _layer_norm(y, g, b):
    mu = jnp.mean(y, axis=-1, keepdims=True)
    d = y - mu
    var = jnp.mean(d * d, axis=-1, keepdims=True)
    return d * lax.rsqrt(var + LN_EPS) * g + b


def _merge_kernel(layer_ref, x_ref, oa_ref, ob_ref, oc_ref, wg_ref, bg_ref, wa_ref, wb_ref, wc_ref, wo_ref, g_ref, b_ref,
                  o_ref):
    for r0 in range(0, x_ref.shape[0], SUB_ROWS):
        rows = slice(r0, r0 + SUB_ROWS)
        xf = x_ref[rows, :]
        xb = xf.astype(BF16)
        merged = None
        for n, (br_ref, w_ref) in enumerate(((oa_ref, wa_ref), (ob_ref, wb_ref), (oc_ref, wc_ref))):
            cols = slice(n * D_MODEL, (n + 1) * D_MODEL)
            z = _dot(xb, wg_ref[:, cols]) + bg_ref[:, cols]
            term = _dot(br_ref[rows, :], w_ref[...]) * (1.0 / (1.0 + jnp.exp(-z)))
            merged = term if merged is None else merged + term
        mix = _dot(merged.astype(BF16), wo_ref[...])
        o_ref[rows, :] = _layer_norm(DEEPNORM_ALPHA * xf + mix, g_ref[...], b_ref[...])


def _merge(layer, x2, oa, ob, oc, w_in, bg, wa, wb, wc, wo, g, b, tm):
    t = x2.shape[0]
    row = lambda w: pl.BlockSpec((tm, w), lambda i, l: (i, 0))
    return pl.pallas_call(
        _merge_kernel,
        grid_spec=pltpu.PrefetchScalarGridSpec(
            num_scalar_prefetch=1,
            grid=(t // tm,),
            in_specs=[row(D_MODEL), row(RET_V_W), row(CA_W), row(DA_V_W),
                      pl.BlockSpec((None, D_MODEL, GATE_W), lambda i, l: (l[0], 0, MIX_W // GATE_W),
                                   pipeline_mode=pl.Buffered(1)),
                      _const_spec((1, GATE_W)),
                      _const_spec((RET_V_W, D_MODEL)), _const_spec((CA_W, D_MODEL)), _const_spec((DA_V_W, D_MODEL)),
                      _const_spec((D_MODEL, D_MODEL)), _const_spec((1, D_MODEL)), _const_spec((1, D_MODEL))],
            out_specs=row(D_MODEL)),
        out_shape=jax.ShapeDtypeStruct((t, D_MODEL), F32),
        compiler_params=_params(1),
        name="merge",
    )(layer, x2, oa, ob, oc, w_in, bg, wa, wb, wc, wo, g, b)


FFN_CHUNK = 256


def _ffn_kernel(x_ref, wi_ref, wo_ref, g_ref, b_ref, o_ref, act_ref):
    for r0 in range(0, x_ref.shape[0], SUB_ROWS):
        rows = slice(r0, r0 + SUB_ROWS)
        xf = x_ref[rows, :]
        xb = xf.astype(BF16)
        for c in range(0, FFN_HIDDEN, FFN_CHUNK):
            ug = _dot(xb, wi_ref[:, c:c + FFN_CHUNK])
            uu = _dot(xb, wi_ref[:, FFN_HIDDEN + c:FFN_HIDDEN + c + FFN_CHUNK])
            act_ref[rows, c:c + FFN_CHUNK] = (ug / (1.0 + jnp.exp(-ug)) * uu).astype(BF16)
        ffn = _dot(act_ref[rows, :], wo_ref[...])
        o_ref[rows, :] = _layer_norm(DEEPNORM_ALPHA * xf + ffn, g_ref[...], b_ref[...])


def _ffn(x2, wi, wo, g, b, tm):
    t = x2.shape[0]
    return pl.pallas_call(
        _ffn_kernel,
        grid=(t // tm,),
        in_specs=[pl.BlockSpec((tm, D_MODEL), lambda i: (i, 0)),
                  _const_spec((D_MODEL, 2 * FFN_HIDDEN)), _const_spec((FFN_HIDDEN, D_MODEL)),
                  _const_spec((1, D_MODEL)), _const_spec((1, D_MODEL))],
        out_specs=pl.BlockSpec((tm, D_MODEL), lambda i: (i, 0)),
        out_shape=jax.ShapeDtypeStruct((t, D_MODEL), F32),
        scratch_shapes=[pltpu.VMEM((tm, FFN_HIDDEN), BF16)],
        compiler_params=_params(1),
        name="ffn",
    )(x2, wi, wo, g, b)


def _rotary_tables(seq):
    pos = jnp.arange(seq, dtype=F32)[:, None]

    def cs(d):
        inv_freq = ROPE_THETA ** (-jnp.arange(0, d, 2, dtype=F32) / d)
        ang = pos * inv_freq[None, :]
        return jnp.cos(ang), jnp.sin(ang)

    rc, rs = cs(RET_DK)
    rcos = jnp.concatenate([rc, rc], axis=1)
    rsin = jnp.concatenate([-rs, rs], axis=1)
    dc, ds = cs(DA_DH)
    z = jnp.zeros_like(ds)
    dcos = jnp.concatenate([dc, dc, dc, dc], axis=1)
    dsina = jnp.concatenate([-ds, z, -ds, z], axis=1)
    dsinb = jnp.concatenate([z, ds, z, ds], axis=1)
    return rcos, rsin, dcos, dsina, dsinb


def kernel(x, w_in, ret_norm_g, ca_rel_bias, da_lambda_q1, da_lambda_k1, da_lambda_q2, da_lambda_k2, da_norm_g, w_branch_a, w_branch_b, w_branch_c, b_merge, w_out, ln1_g, ln1_b, w_ffn_in, w_ffn_out, ln2_g, ln2_b):
    b, s, d = x.shape
    t = b * s
    rot_tabs = _rotary_tables(s)
    ret_tabs = _retention_tables()
    row = lambda a: a.reshape(1, -1).astype(F32)

    assert MIX_W % GATE_W == 0
    w_in_b = w_in.astype(BF16)
    x2 = x.reshape(t, d)
    for l in range(DEPTH):
        lambda_init = 0.8 - 0.6 * math.exp(-0.3 * l)
        layer = jnp.full((1,), l, jnp.int32)
        h = _inproj(layer, x2, w_in_b, rot_tabs, s, PROJ_ROWS)
        h3 = h.reshape(b, s, MIX_W)
        o_a = _retention(h3, ret_tabs, row(ret_norm_g[l]))
        o_b = _chunkattn(h3, _chunkattn_bias(ca_rel_bias[l]))
        o_c = _diffattn(h3, (row(da_lambda_q1[l]), row(da_lambda_k1[l]), row(da_lambda_q2[l]), row(da_lambda_k2[l])),
                        row(da_norm_g[l]), lambda_init)
        x2 = _merge(layer, x2, o_a.reshape(t, RET_V_W), o_b.reshape(t, CA_W), o_c.reshape(t, DA_V_W),
                    w_in_b, row(b_merge[l]), w_branch_a[l].astype(BF16), w_branch_b[l].astype(BF16),
                    w_branch_c[l].astype(BF16), w_out[l].astype(BF16), row(ln1_g[l]), row(ln1_b[l]), TAIL_ROWS)
        x2 = _ffn(x2, w_ffn_in[l].astype(BF16), w_ffn_out[l].astype(BF16), row(ln2_g[l]), row(ln2_b[l]), TAIL_ROWS)
    return x2.reshape(b, s, d)
```

```python
import functools
import math

import jax
import jax.numpy as jnp
import numpy as np
from jax import lax
from jax.experimental import pallas as pl
from jax.experimental.pallas import tpu as pltpu

D_MODEL = 1024
DEPTH = 4
CHUNK = 64
RET_HEADS, RET_DK, RET_DV = 4, 128, 256
CA_HEADS, CA_DH, CA_LEFT_CHUNKS, REL_CLIP = 8, 64, 8, 256
DA_HEADS, DA_DH = 4, 64
FFN_HIDDEN = -(-8 * D_MODEL // (3 * 256)) * 256
ROPE_THETA = 10000.0
LN_EPS = 1e-5
NEG_INF = -1e30
DEEPNORM_ALPHA = (2.0 * DEPTH) ** 0.25

RET_QK_W = RET_HEADS * RET_DK
RET_V_W = RET_HEADS * RET_DV
CA_W = CA_HEADS * CA_DH
DA_QK_W = 2 * DA_HEADS * DA_DH
DA_V_W = DA_HEADS * 2 * DA_DH
OFF_RQ = 0
OFF_RK = OFF_RQ + RET_QK_W
OFF_RV = OFF_RK + RET_QK_W
OFF_RG = OFF_RV + RET_V_W
OFF_CQ = OFF_RG + RET_V_W
OFF_CK = OFF_CQ + CA_W
OFF_CV = OFF_CK + CA_W
OFF_DQ = OFF_CV + CA_W
OFF_DK = OFF_DQ + DA_QK_W
OFF_DV = OFF_DK + DA_QK_W
MIX_W = OFF_DV + DA_V_W
GATE_W = 3 * D_MODEL

LOG2E = math.log2(math.e)
LANES = 128
ONES_ROWS = 16
VMEM_LIMIT = 56 * 1024 * 1024

RET_BLOCK = 256
CA_QBLOCK = 256
CA_HALF = CA_QBLOCK // 2
CA_PAD = CA_LEFT_CHUNKS * CHUNK
CA_WIN = CA_PAD + CA_QBLOCK
CA_HWIN = CA_PAD + CA_HALF
CA_AHEAD = 2
CA_DEPTH = CA_AHEAD + 1
CA_SLAB = CA_DH + ONES_ROWS
DA_BLOCK = 256
DA_PAIRS = 2
DA_AHEAD = 6 * DA_PAIRS
DA_RING = DA_AHEAD + 1
SUB_ROWS = 256
PROJ_ROWS = 512
TAIL_ROWS = 4 * SUB_ROWS

BF16 = jnp.bfloat16
F32 = jnp.float32


def _dot(a, b):
    return jnp.dot(a, b, preferred_element_type=F32)


def _dot_nt(a, b):
    return lax.dot_general(a, b, (((1,), (1,)), ((), ())), preferred_element_type=F32)


def _dot_tn(a, b):
    return lax.dot_general(a, b, (((0,), (0,)), ((), ())), preferred_element_type=F32)


def _const_spec(shape):
    zeros = (0,) * len(shape)
    return pl.BlockSpec(shape, lambda *_: zeros, pipeline_mode=pl.Buffered(1))


def _params(n_axes):
    return pltpu.CompilerParams(dimension_semantics=("arbitrary",) * n_axes, vmem_limit_bytes=VMEM_LIMIT)


def _inproj_kernel(layer_ref, x_ref, w_ref, rcos_ref, rsin_ref, dcos_ref, dsina_ref, dsinb_ref, o_ref):
    xb = x_ref[...].astype(BF16)

    def proj(off, width):
        return _dot(xb, w_ref[:, off:off + width])

    def ret_rotary(a, scale):
        outs = []
        for h in range(RET_HEADS):
            ah = a[:, h * RET_DK:(h + 1) * RET_DK]
            r = ah * rcos_ref[...] + pltpu.roll(ah, RET_DK // 2, 1) * rsin_ref[...]
            outs.append(r * scale if scale != 1.0 else r)
        return jnp.concatenate(outs, axis=1)

    def da_rotary(a, scale):
        outs = []
        for g in range(DA_QK_W // LANES):
            ag = a[:, g * LANES:(g + 1) * LANES]
            r = (ag * dcos_ref[...] + pltpu.roll(ag, LANES - DA_DH // 2, 1) * dsina_ref[...]
                 + pltpu.roll(ag, DA_DH // 2, 1) * dsinb_ref[...])
            outs.append(r * scale if scale != 1.0 else r)
        return jnp.concatenate(outs, axis=1)

    o_ref[:, OFF_RQ:OFF_RQ + RET_QK_W] = ret_rotary(proj(OFF_RQ, RET_QK_W), 1.0).astype(BF16)
    o_ref[:, OFF_RK:OFF_RK + RET_QK_W] = ret_rotary(proj(OFF_RK, RET_QK_W), RET_DK ** -0.5).astype(BF16)
    for off in range(OFF_RV, OFF_CQ, 512):
        o_ref[:, off:off + 512] = proj(off, 512).astype(BF16)
    o_ref[:, OFF_CQ:OFF_CQ + CA_W] = (proj(OFF_CQ, CA_W) * (CA_DH ** -0.5 * LOG2E)).astype(BF16)
    o_ref[:, OFF_CK:OFF_CK + CA_W] = proj(OFF_CK, CA_W).astype(BF16)
    o_ref[:, OFF_CV:OFF_CV + CA_W] = proj(OFF_CV, CA_W).astype(BF16)
    o_ref[:, OFF_DQ:OFF_DQ + DA_QK_W] = da_rotary(proj(OFF_DQ, DA_QK_W), DA_DH ** -0.5 * LOG2E).astype(BF16)
    o_ref[:, OFF_DK:OFF_DK + DA_QK_W] = da_rotary(proj(OFF_DK, DA_QK_W), 1.0).astype(BF16)
    o_ref[:, OFF_DV:OFF_DV + DA_V_W] = proj(OFF_DV, DA_V_W).astype(BF16)


def _inproj(layer, x2, w_in, tabs, seq, tm):
    t = x2.shape[0]
    pos_blocks = seq // tm
    tab_spec = pl.BlockSpec((tm, LANES), lambda i, l: (i % pos_blocks, 0))
    return pl.pallas_call(
        _inproj_kernel,
        grid_spec=pltpu.PrefetchScalarGridSpec(
            num_scalar_prefetch=1,
            grid=(t // tm,),
            in_specs=[pl.BlockSpec((tm, D_MODEL), lambda i, l: (i, 0)),
                      pl.BlockSpec((None, D_MODEL, MIX_W), lambda i, l: (l[0], 0, 0), pipeline_mode=pl.Buffered(1)),
                      tab_spec, tab_spec, tab_spec, tab_spec, tab_spec],
            out_specs=pl.BlockSpec((tm, MIX_W), lambda i, l: (i, 0))),
        out_shape=jax.ShapeDtypeStruct((t, MIX_W), BF16),
        compiler_params=_params(1),
        name="inproj",
    )(layer, x2, w_in, *tabs)


def _retention_kernel(q_ref, k_ref, v_ref, g_ref, dmat_ref, qdec_ref, kdec_ref, cdec_ref, ng_ref, o_ref,
                      state_ref, sv_ref, kv_ref):
    n_blocks = q_ref.shape[0] // RET_BLOCK

    def local(h, c):
        rows = slice(c * RET_BLOCK, (c + 1) * RET_BLOCK)
        q = q_ref[rows, h * RET_DK:(h + 1) * RET_DK]
        k = k_ref[rows, h * RET_DK:(h + 1) * RET_DK]
        v = v_ref[rows, h * RET_DV:(h + 1) * RET_DV]
        s = _dot_nt(q, k) * dmat_ref[h]
        sv_ref[c % 2, h] = _dot(s.astype(BF16), v)
        kd = (k.astype(F32) * kdec_ref[h]).astype(BF16)
        kv_ref[c % 2, h] = _dot_tn(kd, v)

    def finish(h, c):
        rows = slice(c * RET_BLOCK, (c + 1) * RET_BLOCK)
        cols = slice(h * RET_DV, (h + 1) * RET_DV)
        o = sv_ref[c % 2, h]
        if c > 0:
            qd = (q_ref[rows, h * RET_DK:(h + 1) * RET_DK].astype(F32) * qdec_ref[h]).astype(BF16)
            o = o + _dot(qd, state_ref[h].astype(BF16))
            state_ref[h] = state_ref[h] * cdec_ref[h, 0:1, :] + kv_ref[c % 2, h]
        else:
            state_ref[h] = kv_ref[c % 2, h]
        ms = jnp.mean(o * o, axis=-1, keepdims=True)
        y = o * lax.rsqrt(ms + LN_EPS) * ng_ref[...]
        gate = g_ref[rows, cols].astype(F32)
        o_ref[rows, cols] = (gate / (1.0 + jnp.exp(-gate)) * y).astype(BF16)

    for h in range(RET_HEADS):
        local(h, 0)
    for c in range(n_blocks):
        for h in range(RET_HEADS):
            if c + 1 < n_blocks:
                local(h, c + 1)
            finish(h, c)


def _retention(h3, tabs, norm_g):
    b, s, _ = h3.shape
    dmat, qdec, kdec, cdec = tabs
    return pl.pallas_call(
        _retention_kernel,
        grid=(b,),
        in_specs=[pl.BlockSpec((None, s, RET_QK_W), lambda i: (i, 0, OFF_RQ // RET_QK_W)),
                  pl.BlockSpec((None, s, RET_QK_W), lambda i: (i, 0, OFF_RK // RET_QK_W)),
                  pl.BlockSpec((None, s, RET_V_W), lambda i: (i, 0, OFF_RV // RET_V_W)),
                  pl.BlockSpec((None, s, RET_V_W), lambda i: (i, 0, OFF_RG // RET_V_W)),
                  _const_spec((RET_HEADS, RET_BLOCK, RET_BLOCK)),
                  _const_spec((RET_HEADS, RET_BLOCK, RET_DK)),
                  _const_spec((RET_HEADS, RET_BLOCK, RET_DK)),
                  _const_spec((RET_HEADS, 8, RET_DV)),
                  _const_spec((1, RET_DV))],
        out_specs=pl.BlockSpec((None, s, RET_V_W), lambda i: (i, 0, 0)),
        out_shape=jax.ShapeDtypeStruct((b, s, RET_V_W), BF16),
        scratch_shapes=[pltpu.VMEM((RET_HEADS, RET_DK, RET_DV), F32),
                        pltpu.VMEM((2, RET_HEADS, RET_BLOCK, RET_DV), F32),
                        pltpu.VMEM((2, RET_HEADS, RET_DK, RET_DV), F32)],
        compiler_params=_params(1),
        name="retention",
    )(h3, h3, h3, h3, dmat, qdec, kdec, cdec, norm_g)


def _retention_tables():
    h = np.arange(RET_HEADS, dtype=np.float64)
    log_gamma = np.log1p(-np.exp2(-5.0 - h))
    n = np.arange(RET_BLOCK)
    diff = (n[:, None] - n[None, :]).astype(np.float64)
    cn, cm = n[:, None] // CHUNK, n[None, :] // CHUNK
    expo = np.where(cm == cn, np.abs(diff), diff)
    dmat = np.where(cm <= cn, np.exp(log_gamma[:, None, None] * expo[None]), 0.0)
    qdec = np.exp(log_gamma[:, None] * (n[None, :] + 1.0))
    kdec = np.exp(log_gamma[:, None] * (RET_BLOCK - 1.0 - n[None, :]))
    cdec = np.exp(log_gamma * RET_BLOCK)
    qdec = np.broadcast_to(qdec[:, :, None], (RET_HEADS, RET_BLOCK, RET_DK))
    kdec = np.broadcast_to(kdec[:, :, None], (RET_HEADS, RET_BLOCK, RET_DK))
    cdec = np.broadcast_to(cdec[:, None, None], (RET_HEADS, 8, RET_DV))
    return tuple(jnp.asarray(a, F32) for a in (dmat, qdec, kdec, cdec))


def _chunkattn_kernel(q_ref, k_ref, v_ref, bias_ref, o_ref, kpad_ref, vt_ref, s_ref, p_ref, ot_ref):
    s_len = q_ref.shape[0]
    n_blocks = s_len // CA_QBLOCK
    n_pad = CA_PAD // CA_QBLOCK
    kpad_ref[0:CA_PAD, :] = jnp.zeros((CA_PAD, CA_W), BF16)
    kpad_ref[CA_PAD:, :] = k_ref[...]
    ones = jnp.ones((ONES_ROWS, CA_QBLOCK), BF16)
    for c in range(n_pad + n_blocks):
        if c < n_pad:
            vt = jnp.zeros((CA_W, CA_QBLOCK), BF16)
        else:
            vt = v_ref[(c - n_pad) * CA_QBLOCK:(c - n_pad + 1) * CA_QBLOCK, :].astype(F32).T.astype(BF16)
        for h in range(CA_HEADS):
            vt_ref[c, h * CA_SLAB:h * CA_SLAB + CA_DH, :] = vt[h * CA_DH:(h + 1) * CA_DH, :]
            vt_ref[c, h * CA_SLAB + CA_DH:(h + 1) * CA_SLAB, :] = ones
    p_ref[:, CA_HWIN:CA_WIN, 0:CA_HALF] = jnp.zeros((2, CA_WIN - CA_HWIN, CA_HALF), BF16)
    p_ref[:, 0:CA_HALF, CA_HALF:CA_QBLOCK] = jnp.zeros((2, CA_HALF, CA_HALF), BF16)
    lane = lax.broadcasted_iota(jnp.int32, (CA_QBLOCK, LANES), 1)
    head_lanes = (lane < CA_DH, lane >= CA_DH)
    win_row = lax.broadcasted_iota(jnp.int32, (CA_HWIN, CA_HALF), 0)

    def scores(n):
        i, h = divmod(n, CA_HEADS)
        pair = slice((h // 2) * LANES, (h // 2 + 1) * LANES)
        q = q_ref[i * CA_QBLOCK:(i + 1) * CA_QBLOCK, pair]
        qz = jnp.where(head_lanes[h % 2], q, jnp.zeros_like(q))
        s_ref[n % CA_DEPTH] = _dot_nt(kpad_ref[i * CA_QBLOCK:i * CA_QBLOCK + CA_WIN, pair], qz)

    def attend(n):
        i, h = divmod(n, CA_HEADS)
        first_key = i * CA_QBLOCK - CA_PAD
        for half in range(2):
            rows = slice(half * CA_HALF, half * CA_HALF + CA_HWIN)
            cols = slice(half * CA_HALF, (half + 1) * CA_HALF)
            s = s_ref[n % CA_DEPTH, rows, cols] + bias_ref[h]
            if first_key + half * CA_HALF < 0:
                s = jnp.where(win_row >= -(first_key + half * CA_HALF), s, NEG_INF)
            m = jnp.max(s, axis=0, keepdims=True)
            p_ref[h % 2, rows, cols] = jnp.exp2(s - m).astype(BF16)
        pv = None
        for c in range(CA_WIN // CA_QBLOCK):
            part = _dot(vt_ref[i + c, h * CA_SLAB:(h + 1) * CA_SLAB, :],
                        p_ref[h % 2, c * CA_QBLOCK:(c + 1) * CA_QBLOCK, :])
            pv = part if pv is None else pv + part
        ot_ref[h * CA_DH:(h + 1) * CA_DH, :] = pv[0:CA_DH] * (1.0 / pv[CA_DH:CA_DH + 1])
        if h == CA_HEADS - 1:
            o_ref[i * CA_QBLOCK:(i + 1) * CA_QBLOCK, :] = ot_ref[...].T.astype(BF16)

    n_steps = n_blocks * CA_HEADS
    for n in range(n_steps + CA_AHEAD):
        if n < n_steps:
            scores(n)
        if n >= CA_AHEAD:
            attend(n - CA_AHEAD)


def _chunkattn(h3, bias):
    b, s, _ = h3.shape
    n_blocks = s // CA_QBLOCK
    return pl.pallas_call(
        _chunkattn_kernel,
        grid=(b,),
        in_specs=[pl.BlockSpec((None, s, CA_W), lambda i: (i, 0, OFF_CQ // CA_W)),
                  pl.BlockSpec((None, s, CA_W), lambda i: (i, 0, OFF_CK // CA_W)),
                  pl.BlockSpec((None, s, CA_W), lambda i: (i, 0, OFF_CV // CA_W)),
                  _const_spec((CA_HEADS, CA_HWIN, CA_HALF))],
        out_specs=pl.BlockSpec((None, s, CA_W), lambda i: (i, 0, 0)),
        out_shape=jax.ShapeDtypeStruct((b, s, CA_W), BF16),
        scratch_shapes=[pltpu.VMEM((s + CA_PAD, CA_W), BF16),
                        pltpu.VMEM((n_blocks + CA_PAD // CA_QBLOCK, CA_HEADS * CA_SLAB, CA_QBLOCK), BF16),
                        pltpu.VMEM((CA_DEPTH, CA_WIN, CA_QBLOCK), F32),
                        pltpu.VMEM((2, CA_WIN, CA_QBLOCK), BF16),
                        pltpu.VMEM((CA_W, CA_QBLOCK), F32)],
        compiler_params=_params(1),
        name="chunkattn",
    )(h3, h3, h3, bias)


def _chunkattn_bias(rel_bias):
    j = np.arange(CA_HWIN)[:, None]
    r = np.arange(CA_HALF)[None, :]
    qc, kc = r // CHUNK, j // CHUNK
    band = (kc >= qc) & (kc <= qc + CA_LEFT_CHUNKS)
    n_diag = CA_HWIN + CA_HALF
    u = np.arange(n_diag)
    d = np.where(u < CA_HALF, u, u - n_diag)
    idx = np.clip(d + CA_PAD, -REL_CLIP, REL_CLIP) + REL_CLIP
    diag = rel_bias[:, idx].astype(F32) * LOG2E
    shifted = jnp.tile(diag, (1, CA_HWIN))[:, :CA_HWIN * (n_diag - 1)].reshape(CA_HEADS, CA_HWIN, n_diag - 1)
    table = shifted[:, :, :CA_HALF]
    return jnp.where(jnp.asarray(band)[None], table, NEG_INF)


def _diffattn_kernel(q_ref, k_ref, v_ref, lq1_ref, lk1_ref, lq2_ref, lk2_ref, linit_ref, ng_ref, o_ref,
                     vt_ref, s_ref, pv_ref):
    s_len = q_ref.shape[0]
    n_blocks = s_len // DA_BLOCK
    hw = 2 * DA_DH
    lam = (jnp.exp(jnp.sum(lq1_ref[...] * lk1_ref[...], axis=-1, keepdims=True))
           - jnp.exp(jnp.sum(lq2_ref[...] * lk2_ref[...], axis=-1, keepdims=True)) + linit_ref[...])
    out_scale = ng_ref[...] * (1.0 - linit_ref[...])
    for c in range(n_blocks):
        blk = slice(c * DA_BLOCK, (c + 1) * DA_BLOCK)
        vt = v_ref[blk, :].astype(F32).T.astype(BF16)
        for pr in range(DA_PAIRS):
            vt_ref[pr, 0:hw, blk] = vt[pr * hw:(pr + 1) * hw, :]
    vt_ref[:, hw:, :] = jnp.ones((DA_PAIRS, ONES_ROWS, s_len), BF16)
    key_chunk = lax.broadcasted_iota(jnp.int32, (DA_BLOCK, DA_BLOCK), 0) // CHUNK
    query_chunk = lax.broadcasted_iota(jnp.int32, (DA_BLOCK, DA_BLOCK), 1) // CHUNK
    diag_allowed = key_chunk <= query_chunk
    lane = lax.broadcasted_iota(jnp.int32, (DA_BLOCK, hw), 1)
    head_lanes = (lane < DA_DH, lane >= DA_DH)

    steps = [(i, t, c, pr) for i in range(n_blocks) for t in range(2) for c in range(i + 1) for pr in range(DA_PAIRS)]
    maxima = {}

    def scores(n):
        i, t, c, pr = steps[n]
        lanes = slice(pr * hw, (pr + 1) * hw)
        q = q_ref[i * DA_BLOCK:(i + 1) * DA_BLOCK, lanes]
        qz = jnp.where(head_lanes[t], q, jnp.zeros_like(q))
        s = _dot_nt(k_ref[c * DA_BLOCK:(c + 1) * DA_BLOCK, lanes], qz)
        if c == i:
            s = jnp.where(diag_allowed, s, NEG_INF)
        s_ref[n % DA_RING] = s
        maxima[n] = jnp.max(s, axis=0, keepdims=True)

    def attend(n):
        i, t, c, pr = steps[n]
        p = jnp.exp2(s_ref[n % DA_RING] - maxima[n]).astype(BF16)
        pv_ref[pr, i % 2, t, c] = _dot(vt_ref[pr, :, c * DA_BLOCK:(c + 1) * DA_BLOCK], p)
        if c == i:
            mine = [n - (i - j) * DA_PAIRS for j in range(i + 1)]
            m = functools.reduce(jnp.maximum, [maxima[j] for j in mine])
            acc = None
            for j, nj in enumerate(mine):
                part = pv_ref[pr, i % 2, t, j] * jnp.exp2(maxima.pop(nj) - m)
                acc = part if acc is None else acc + part
            heads[pr][t] = acc[0:hw] * (1.0 / acc[hw:hw + 1])
            if t == 1:
                o_t = heads[pr][0] - lam * heads[pr][1]
                ms = jnp.mean(o_t * o_t, axis=0, keepdims=True)
                y = (o_t * lax.rsqrt(ms + LN_EPS)).T * out_scale
                o_ref[i * DA_BLOCK:(i + 1) * DA_BLOCK, pr * hw:(pr + 1) * hw] = y.astype(BF16)

    heads = [[None, None] for _ in range(DA_PAIRS)]
    for n in range(len(steps) + DA_AHEAD):
        if n < len(steps):
            scores(n)
        if n >= DA_AHEAD:
            attend(n - DA_AHEAD)


def _diffattn(h3, lams, norm_g, lambda_init):
    b, s, _ = h3.shape
    hw = 2 * DA_DH
    bw = DA_PAIRS * hw
    small = pl.BlockSpec((1, DA_DH), lambda i, h: (0, 0))
    return pl.pallas_call(
        _diffattn_kernel,
        grid=(b, DA_HEADS // DA_PAIRS),
        in_specs=[pl.BlockSpec((None, s, bw), lambda i, h: (i, 0, OFF_DQ // bw + h)),
                  pl.BlockSpec((None, s, bw), lambda i, h: (i, 0, OFF_DK // bw + h)),
                  pl.BlockSpec((None, s, bw), lambda i, h: (i, 0, OFF_DV // bw + h)),
                  small, small, small, small,
                  pl.BlockSpec((1, 1), lambda i, h: (0, 0)),
                  pl.BlockSpec((1, hw), lambda i, h: (0, 0))],
        out_specs=pl.BlockSpec((None, s, bw), lambda i, h: (i, 0, h)),
        out_shape=jax.ShapeDtypeStruct((b, s, DA_V_W), BF16),
        scratch_shapes=[pltpu.VMEM((DA_PAIRS, hw + ONES_ROWS, s), BF16),
                        pltpu.VMEM((DA_RING, DA_BLOCK, DA_BLOCK), F32),
                        pltpu.VMEM((DA_PAIRS, 2, 2, s // DA_BLOCK, hw + ONES_ROWS, DA_BLOCK), F32)],
        compiler_params=_params(2),
        name="diffattn",
    )(h3, h3, h3, *lams, jnp.full((1, 1), lambda_init, F32), norm_g)


def _layer_norm(y, g, b):
    mu = jnp.mean(y, axis=-1, keepdims=True)
    d = y - mu
    var = jnp.mean(d * d, axis=-1, keepdims=True)
    return d * lax.rsqrt(var + LN_EPS) * g + b


def _merge_kernel(layer_ref, x_ref, oa_ref, ob_ref, oc_ref, wg_ref, bg_ref, wa_ref, wb_ref, wc_ref, wo_ref, g_ref, b_ref,
                  o_ref):
    for r0 in range(0, x_ref.shape[0], SUB_ROWS):
        rows = slice(r0, r0 + SUB_ROWS)
        xf = x_ref[rows, :]
        xb = xf.astype(BF16)
        merged = None
        for n, (br_ref, w_ref) in enumerate(((oa_ref, wa_ref), (ob_ref, wb_ref), (oc_ref, wc_ref))):
            cols = slice(n * D_MODEL, (n + 1) * D_MODEL)
            z = _dot(xb, wg_ref[:, cols]) + bg_ref[:, cols]
            term = _dot(br_ref[rows, :], w_ref[...]) * (1.0 / (1.0 + jnp.exp(-z)))
            merged = term if merged is None else merged + term
        mix = _dot(merged.astype(BF16), wo_ref[...])
        o_ref[rows, :] = _layer_norm(DEEPNORM_ALPHA * xf + mix, g_ref[...], b_ref[...])


def _merge(layer, x2, oa, ob, oc, w_in, bg, wa, wb, wc, wo, g, b, tm):
    t = x2.shape[0]
    row = lambda w: pl.BlockSpec((tm, w), lambda i, l: (i, 0))
    return pl.pallas_call(
        _merge_kernel,
        grid_spec=pltpu.PrefetchScalarGridSpec(
            num_scalar_prefetch=1,
            grid=(t // tm,),
            in_specs=[row(D_MODEL), row(RET_V_W), row(CA_W), row(DA_V_W),
                      pl.BlockSpec((None, D_MODEL, GATE_W), lambda i, l: (l[0], 0, MIX_W // GATE_W),
                                   pipeline_mode=pl.Buffered(1)),
                      _const_spec((1, GATE_W)),
                      _const_spec((RET_V_W, D_MODEL)), _const_spec((CA_W, D_MODEL)), _const_spec((DA_V_W, D_MODEL)),
                      _const_spec((D_MODEL, D_MODEL)), _const_spec((1, D_MODEL)), _const_spec((1, D_MODEL))],
            out_specs=row(D_MODEL)),
        out_shape=jax.ShapeDtypeStruct((t, D_MODEL), F32),
        compiler_params=_params(1),
        name="merge",
    )(layer, x2, oa, ob, oc, w_in, bg, wa, wb, wc, wo, g, b)


FFN_CHUNK = 256


def _ffn_kernel(x_ref, wi_ref, wo_ref, g_ref, b_ref, o_ref, act_ref):
    for r0 in range(0, x_ref.shape[0], SUB_ROWS):
        rows = slice(r0, r0 + SUB_ROWS)
        xf = x_ref[rows, :]
        xb = xf.astype(BF16)
        for c in range(0, FFN_HIDDEN, FFN_CHUNK):
            ug = _dot(xb, wi_ref[:, c:c + FFN_CHUNK])
            uu = _dot(xb, wi_ref[:, FFN_HIDDEN + c:FFN_HIDDEN + c + FFN_CHUNK])
            act_ref[rows, c:c + FFN_CHUNK] = (ug / (1.0 + jnp.exp(-ug)) * uu).astype(BF16)
        ffn = _dot(act_ref[rows, :], wo_ref[...])
        o_ref[rows, :] = _layer_norm(DEEPNORM_ALPHA * xf + ffn, g_ref[...], b_ref[...])


def _ffn(x2, wi, wo, g, b, tm):
    t = x2.shape[0]
    return pl.pallas_call(
        _ffn_kernel,
        grid=(t // tm,),
        in_specs=[pl.BlockSpec((tm, D_MODEL), lambda i: (i, 0)),
                  _const_spec((D_MODEL, 2 * FFN_HIDDEN)), _const_spec((FFN_HIDDEN, D_MODEL)),
                  _const_spec((1, D_MODEL)), _const_spec((1, D_MODEL))],
        out_specs=pl.BlockSpec((tm, D_MODEL), lambda i: (i, 0)),
        out_shape=jax.ShapeDtypeStruct((t, D_MODEL), F32),
        scratch_shapes=[pltpu.VMEM((tm, FFN_HIDDEN), BF16)],
        compiler_params=_params(1),
        name="ffn",
    )(x2, wi, wo, g, b)


def _rotary_tables(seq):
    pos = jnp.arange(seq, dtype=F32)[:, None]

    def cs(d):
        inv_freq = ROPE_THETA ** (-jnp.arange(0, d, 2, dtype=F32) / d)
        ang = pos * inv_freq[None, :]
        return jnp.cos(ang), jnp.sin(ang)

    rc, rs = cs(RET_DK)
    rcos = jnp.concatenate([rc, rc], axis=1)
    rsin = jnp.concatenate([-rs, rs], axis=1)
    dc, ds = cs(DA_DH)
    z = jnp.zeros_like(ds)
    dcos = jnp.concatenate([dc, dc, dc, dc], axis=1)
    dsina = jnp.concatenate([-ds, z, -ds, z], axis=1)
    dsinb = jnp.concatenate([z, ds, z, ds], axis=1)
    return rcos, rsin, dcos, dsina, dsinb


def kernel(x, w_in, ret_norm_g, ca_rel_bias, da_lambda_q1, da_lambda_k1, da_lambda_q2, da_lambda_k2, da_norm_g, w_branch_a, w_branch_b, w_branch_c, b_merge, w_out, ln1_g, ln1_b, w_ffn_in, w_ffn_out, ln2_g, ln2_b):
    b, s, d = x.shape
    t = b * s
    rot_tabs = _rotary_tables(s)
    ret_tabs = _retention_tables()
    row = lambda a: a.reshape(1, -1).astype(F32)

    assert MIX_W % GATE_W == 0
    w_in_b = w_in.astype(BF16)
    x2 = x.reshape(t, d)
    for l in range(DEPTH):
        lambda_init = 0.8 - 0.6 * math.exp(-0.3 * l)
        layer = jnp.full((1,), l, jnp.int32)
        h = _inproj(layer, x2, w_in_b, rot_tabs, s, PROJ_ROWS)
        h3 = h.reshape(b, s, MIX_W)
        o_a = _retention(h3, ret_tabs, row(ret_norm_g[l]))
        o_b = _chunkattn(h3, _chunkattn_bias(ca_rel_bias[l]))
        o_c = _diffattn(h3, (row(da_lambda_q1[l]), row(da_lambda_k1[l]), row(da_lambda_q2[l]), row(da_lambda_k2[l])),
                        row(da_norm_g[l]), lambda_init)
        x2 = _merge(layer, x2, o_a.reshape(t, RET_V_W), o_b.reshape(t, CA_W), o_c.reshape(t, DA_V_W),
                    w_in_b, row(b_merge[l]), w_branch_a[l].astype(BF16), w_branch_b[l].astype(BF16),
                    w_branch_c[l].astype(BF16), w_out[l].astype(BF16), row(ln1_g[l]), row(ln1_b[l]), TAIL_ROWS)
        x2 = _ffn(x2, w_ffn_in[l].astype(BF16), w_ffn_out[l].astype(BF16), row(ln2_g[l]), row(ln2_b[l]), TAIL_ROWS)
    return x2.reshape(b, s, d)
```

```python
import functools
import math

import jax
import jax.numpy as jnp
import numpy as np
from jax import lax
from jax.experimental import pallas as pl
from jax.experimental.pallas import tpu as pltpu

D_MODEL = 1024
DEPTH = 4
CHUNK = 64
RET_HEADS, RET_DK, RET_DV = 4, 128, 256
CA_HEADS, CA_DH, CA_LEFT_CHUNKS, REL_CLIP = 8, 64, 8, 256
DA_HEADS, DA_DH = 4, 64
FFN_HIDDEN = -(-8 * D_MODEL // (3 * 256)) * 256
ROPE_THETA = 10000.0
LN_EPS = 1e-5
NEG_INF = -1e30
DEEPNORM_ALPHA = (2.0 * DEPTH) ** 0.25

RET_QK_W = RET_HEADS * RET_DK
RET_V_W = RET_HEADS * RET_DV
CA_W = CA_HEADS * CA_DH
DA_QK_W = 2 * DA_HEADS * DA_DH
DA_V_W = DA_HEADS * 2 * DA_DH
OFF_RQ = 0
OFF_RK = OFF_RQ + RET_QK_W
OFF_RV = OFF_RK + RET_QK_W
OFF_RG = OFF_RV + RET_V_W
OFF_CQ = OFF_RG + RET_V_W
OFF_CK = OFF_CQ + CA_W
OFF_CV = OFF_CK + CA_W
OFF_DQ = OFF_CV + CA_W
OFF_DK = OFF_DQ + DA_QK_W
OFF_DV = OFF_DK + DA_QK_W
MIX_W = OFF_DV + DA_V_W
GATE_W = 3 * D_MODEL

LOG2E = math.log2(math.e)
LANES = 128
SUBLANES = 8
PROJ_COLS = 512
ONES_ROWS = 16
VMEM_LIMIT = 56 * 1024 * 1024

RET_BLOCK = 256
CA_QBLOCK = 256
CA_HALF = CA_QBLOCK // 2
CA_PAD = CA_LEFT_CHUNKS * CHUNK
CA_WIN = CA_PAD + CA_QBLOCK
CA_HWIN = CA_PAD + CA_HALF
CA_AHEAD = 2
CA_DEPTH = CA_AHEAD + 1
CA_SLAB = CA_DH + ONES_ROWS
DA_BLOCK = 256
DA_PAIRS = 2
DA_AHEAD = 6 * DA_PAIRS
DA_RING = DA_AHEAD + 1
SUB_ROWS = 256
PROJ_ROWS = 512
TAIL_ROWS = 4 * SUB_ROWS
FFN_CHUNK = 256

BF16 = jnp.bfloat16
F32 = jnp.float32


def _dot(a, b):
    return jnp.dot(a, b, preferred_element_type=F32)


def _dot_nt(a, b):
    return lax.dot_general(a, b, (((1,), (1,)), ((), ())), preferred_element_type=F32)


def _dot_tn(a, b):
    return lax.dot_general(a, b, (((0,), (0,)), ((), ())), preferred_element_type=F32)


def _const_spec(shape):
    zeros = (0,) * len(shape)
    return pl.BlockSpec(shape, lambda *_: zeros, pipeline_mode=pl.Buffered(1))


def _params(n_axes):
    return pltpu.CompilerParams(dimension_semantics=("arbitrary",) * n_axes, vmem_limit_bytes=VMEM_LIMIT)


def _inproj_kernel(layer_ref, x_ref, w_ref, rcos_ref, rsin_ref, dcos_ref, dsina_ref, dsinb_ref, o_ref):
    xb = x_ref[...].astype(BF16)

    def proj(off, width):
        return _dot(xb, w_ref[:, off:off + width])

    def ret_rotary(a, scale):
        outs = []
        for h in range(RET_HEADS):
            ah = a[:, h * RET_DK:(h + 1) * RET_DK]
            r = ah * rcos_ref[...] + pltpu.roll(ah, RET_DK // 2, 1) * rsin_ref[...]
            outs.append(r * scale if scale != 1.0 else r)
        return jnp.concatenate(outs, axis=1)

    def da_rotary(a, scale):
        outs = []
        for g in range(DA_QK_W // LANES):
            ag = a[:, g * LANES:(g + 1) * LANES]
            r = (ag * dcos_ref[...] + pltpu.roll(ag, LANES - DA_DH // 2, 1) * dsina_ref[...]
                 + pltpu.roll(ag, DA_DH // 2, 1) * dsinb_ref[...])
            outs.append(r * scale if scale != 1.0 else r)
        return jnp.concatenate(outs, axis=1)

    o_ref[:, OFF_RQ:OFF_RQ + RET_QK_W] = ret_rotary(proj(OFF_RQ, RET_QK_W), 1.0).astype(BF16)
    o_ref[:, OFF_RK:OFF_RK + RET_QK_W] = ret_rotary(proj(OFF_RK, RET_QK_W), RET_DK ** -0.5).astype(BF16)
    for off in range(OFF_RV, OFF_CQ, PROJ_COLS):
        o_ref[:, off:off + PROJ_COLS] = proj(off, PROJ_COLS).astype(BF16)
    o_ref[:, OFF_CQ:OFF_CQ + CA_W] = (proj(OFF_CQ, CA_W) * (CA_DH ** -0.5 * LOG2E)).astype(BF16)
    o_ref[:, OFF_CK:OFF_CK + CA_W] = proj(OFF_CK, CA_W).astype(BF16)
    o_ref[:, OFF_CV:OFF_CV + CA_W] = proj(OFF_CV, CA_W).astype(BF16)
    o_ref[:, OFF_DQ:OFF_DQ + DA_QK_W] = da_rotary(proj(OFF_DQ, DA_QK_W), DA_DH ** -0.5 * LOG2E).astype(BF16)
    o_ref[:, OFF_DK:OFF_DK + DA_QK_W] = da_rotary(proj(OFF_DK, DA_QK_W), 1.0).astype(BF16)
    o_ref[:, OFF_DV:OFF_DV + DA_V_W] = proj(OFF_DV, DA_V_W).astype(BF16)


def _inproj(layer, x2, w_in, tabs, seq, tm):
    t = x2.shape[0]
    pos_blocks = seq // tm
    tab_spec = pl.BlockSpec((tm, LANES), lambda i, l: (i % pos_blocks, 0))
    return pl.pallas_call(
        _inproj_kernel,
        grid_spec=pltpu.PrefetchScalarGridSpec(
            num_scalar_prefetch=1,
            grid=(t // tm,),
            in_specs=[pl.BlockSpec((tm, D_MODEL), lambda i, l: (i, 0)),
                      pl.BlockSpec((None, D_MODEL, MIX_W), lambda i, l: (l[0], 0, 0), pipeline_mode=pl.Buffered(1)),
                      tab_spec, tab_spec, tab_spec, tab_spec, tab_spec],
            out_specs=pl.BlockSpec((tm, MIX_W), lambda i, l: (i, 0))),
        out_shape=jax.ShapeDtypeStruct((t, MIX_W), BF16),
        compiler_params=_params(1),
        name="inproj",
    )(layer, x2, w_in, *tabs)


def _retention_kernel(q_ref, k_ref, v_ref, g_ref, dmat_ref, qdec_ref, kdec_ref, cdec_ref, ng_ref, o_ref,
                      state_ref, sv_ref, kv_ref):
    n_blocks = q_ref.shape[0] // RET_BLOCK

    def local(h, c):
        rows = slice(c * RET_BLOCK, (c + 1) * RET_BLOCK)
        q = q_ref[rows, h * RET_DK:(h + 1) * RET_DK]
        k = k_ref[rows, h * RET_DK:(h + 1) * RET_DK]
        v = v_ref[rows, h * RET_DV:(h + 1) * RET_DV]
        s = _dot_nt(q, k) * dmat_ref[h]
        sv_ref[c % 2, h] = _dot(s.astype(BF16), v)
        kd = (k.astype(F32) * kdec_ref[h]).astype(BF16)
        kv_ref[c % 2, h] = _dot_tn(kd, v)

    def finish(h, c):
        rows = slice(c * RET_BLOCK, (c + 1) * RET_BLOCK)
        cols = slice(h * RET_DV, (h + 1) * RET_DV)
        o = sv_ref[c % 2, h]
        if c > 0:
            qd = (q_ref[rows, h * RET_DK:(h + 1) * RET_DK].astype(F32) * qdec_ref[h]).astype(BF16)
            o = o + _dot(qd, state_ref[h].astype(BF16))
            state_ref[h] = state_ref[h] * cdec_ref[h, 0:1, :] + kv_ref[c % 2, h]
        else:
            state_ref[h] = kv_ref[c % 2, h]
        ms = jnp.mean(o * o, axis=-1, keepdims=True)
        y = o * lax.rsqrt(ms + LN_EPS) * ng_ref[...]
        gate = g_ref[rows, cols].astype(F32)
        o_ref[rows, cols] = (gate / (1.0 + jnp.exp(-gate)) * y).astype(BF16)

    for h in range(RET_HEADS):
        local(h, 0)
    for c in range(n_blocks):
        for h in range(RET_HEADS):
            if c + 1 < n_blocks:
                local(h, c + 1)
            finish(h, c)


def _retention(h3, tabs, norm_g):
    b, s, _ = h3.shape
    dmat, qdec, kdec, cdec = tabs
    return pl.pallas_call(
        _retention_kernel,
        grid=(b,),
        in_specs=[pl.BlockSpec((None, s, RET_QK_W), lambda i: (i, 0, OFF_RQ // RET_QK_W)),
                  pl.BlockSpec((None, s, RET_QK_W), lambda i: (i, 0, OFF_RK // RET_QK_W)),
                  pl.BlockSpec((None, s, RET_V_W), lambda i: (i, 0, OFF_RV // RET_V_W)),
                  pl.BlockSpec((None, s, RET_V_W), lambda i: (i, 0, OFF_RG // RET_V_W)),
                  _const_spec((RET_HEADS, RET_BLOCK, RET_BLOCK)),
                  _const_spec((RET_HEADS, RET_BLOCK, RET_DK)),
                  _const_spec((RET_HEADS, RET_BLOCK, RET_DK)),
                  _const_spec((RET_HEADS, SUBLANES, RET_DV)),
                  _const_spec((1, RET_DV))],
        out_specs=pl.BlockSpec((None, s, RET_V_W), lambda i: (i, 0, 0)),
        out_shape=jax.ShapeDtypeStruct((b, s, RET_V_W), BF16),
        scratch_shapes=[pltpu.VMEM((RET_HEADS, RET_DK, RET_DV), F32),
                        pltpu.VMEM((2, RET_HEADS, RET_BLOCK, RET_DV), F32),
                        pltpu.VMEM((2, RET_HEADS, RET_DK, RET_DV), F32)],
        compiler_params=_params(1),
        name="retention",
    )(h3, h3, h3, h3, dmat, qdec, kdec, cdec, norm_g)


def _retention_tables():
    h = np.arange(RET_HEADS, dtype=np.float64)
    log_gamma = np.log1p(-np.exp2(-5.0 - h))
    n = np.arange(RET_BLOCK)
    diff = (n[:, None] - n[None, :]).astype(np.float64)
    cn, cm = n[:, None] // CHUNK, n[None, :] // CHUNK
    expo = np.where(cm == cn, np.abs(diff), diff)
    dmat = np.where(cm <= cn, np.exp(log_gamma[:, None, None] * expo[None]), 0.0)
    qdec = np.exp(log_gamma[:, None] * (n[None, :] + 1.0))
    kdec = np.exp(log_gamma[:, None] * (RET_BLOCK - 1.0 - n[None, :]))
    cdec = np.exp(log_gamma * RET_BLOCK)
    qdec = np.broadcast_to(qdec[:, :, None], (RET_HEADS, RET_BLOCK, RET_DK))
    kdec = np.broadcast_to(kdec[:, :, None], (RET_HEADS, RET_BLOCK, RET_DK))
    cdec = np.broadcast_to(cdec[:, None, None], (RET_HEADS, SUBLANES, RET_DV))
    return tuple(jnp.asarray(a, F32) for a in (dmat, qdec, kdec, cdec))


def _chunkattn_kernel(q_ref, k_ref, v_ref, bias_ref, o_ref, kpad_ref, vt_ref, s_ref, p_ref, ot_ref):
    s_len = q_ref.shape[0]
    n_blocks = s_len // CA_QBLOCK
    n_pad = CA_PAD // CA_QBLOCK
    kpad_ref[0:CA_PAD, :] = jnp.zeros((CA_PAD, CA_W), BF16)
    kpad_ref[CA_PAD:, :] = k_ref[...]
    ones = jnp.ones((ONES_ROWS, CA_QBLOCK), BF16)
    for c in range(n_pad + n_blocks):
        if c < n_pad:
            vt = jnp.zeros((CA_W, CA_QBLOCK), BF16)
        else:
            vt = v_ref[(c - n_pad) * CA_QBLOCK:(c - n_pad + 1) * CA_QBLOCK, :].astype(F32).T.astype(BF16)
        for h in range(CA_HEADS):
            vt_ref[c, h * CA_SLAB:h * CA_SLAB + CA_DH, :] = vt[h * CA_DH:(h + 1) * CA_DH, :]
            vt_ref[c, h * CA_SLAB + CA_DH:(h + 1) * CA_SLAB, :] = ones
    p_ref[:, CA_HWIN:CA_WIN, 0:CA_HALF] = jnp.zeros((2, CA_WIN - CA_HWIN, CA_HALF), BF16)
    p_ref[:, 0:CA_HALF, CA_HALF:CA_QBLOCK] = jnp.zeros((2, CA_HALF, CA_HALF), BF16)
    lane = lax.broadcasted_iota(jnp.int32, (CA_QBLOCK, LANES), 1)
    head_lanes = (lane < CA_DH, lane >= CA_DH)
    win_row = lax.broadcasted_iota(jnp.int32, (CA_HWIN, CA_HALF), 0)

    def scores(n):
        i, h = divmod(n, CA_HEADS)
        pair = slice((h // 2) * LANES, (h // 2 + 1) * LANES)
        q = q_ref[i * CA_QBLOCK:(i + 1) * CA_QBLOCK, pair]
        qz = jnp.where(head_lanes[h % 2], q, jnp.zeros_like(q))
        s_ref[n % CA_DEPTH] = _dot_nt(kpad_ref[i * CA_QBLOCK:i * CA_QBLOCK + CA_WIN, pair], qz)

    def attend(n):
        i, h = divmod(n, CA_HEADS)
        first_key = i * CA_QBLOCK - CA_PAD
        for half in range(2):
            rows = slice(half * CA_HALF, half * CA_HALF + CA_HWIN)
            cols = slice(half * CA_HALF, (half + 1) * CA_HALF)
            s = s_ref[n % CA_DEPTH, rows, cols] + bias_ref[h]
            if first_key + half * CA_HALF < 0:
                s = jnp.where(win_row >= -(first_key + half * CA_HALF), s, NEG_INF)
            m = jnp.max(s, axis=0, keepdims=True)
            p_ref[h % 2, rows, cols] = jnp.exp2(s - m).astype(BF16)
        pv = None
        for c in range(CA_WIN // CA_QBLOCK):
            part = _dot(vt_ref[i + c, h * CA_SLAB:(h + 1) * CA_SLAB, :],
                        p_ref[h % 2, c * CA_QBLOCK:(c + 1) * CA_QBLOCK, :])
            pv = part if pv is None else pv + part
        ot_ref[h * CA_DH:(h + 1) * CA_DH, :] = pv[0:CA_DH] * (1.0 / pv[CA_DH:CA_DH + 1])
        if h == CA_HEADS - 1:
            o_ref[i * CA_QBLOCK:(i + 1) * CA_QBLOCK, :] = ot_ref[...].T.astype(BF16)

    n_steps = n_blocks * CA_HEADS
    for n in range(n_steps + CA_AHEAD):
        if n < n_steps:
            scores(n)
        if n >= CA_AHEAD:
            attend(n - CA_AHEAD)


def _chunkattn(h3, bias):
    b, s, _ = h3.shape
    n_blocks = s // CA_QBLOCK
    return pl.pallas_call(
        _chunkattn_kernel,
        grid=(b,),
        in_specs=[pl.BlockSpec((None, s, CA_W), lambda i: (i, 0, OFF_CQ // CA_W)),
                  pl.BlockSpec((None, s, CA_W), lambda i: (i, 0, OFF_CK // CA_W)),
                  pl.BlockSpec((None, s, CA_W), lambda i: (i, 0, OFF_CV // CA_W)),
                  _const_spec((CA_HEADS, CA_HWIN, CA_HALF))],
        out_specs=pl.BlockSpec((None, s, CA_W), lambda i: (i, 0, 0)),
        out_shape=jax.ShapeDtypeStruct((b, s, CA_W), BF16),
        scratch_shapes=[pltpu.VMEM((s + CA_PAD, CA_W), BF16),
                        pltpu.VMEM((n_blocks + CA_PAD // CA_QBLOCK, CA_HEADS * CA_SLAB, CA_QBLOCK), BF16),
                        pltpu.VMEM((CA_DEPTH, CA_WIN, CA_QBLOCK), F32),
                        pltpu.VMEM((2, CA_WIN, CA_QBLOCK), BF16),
                        pltpu.VMEM((CA_W, CA_QBLOCK), F32)],
        compiler_params=_params(1),
        name="chunkattn",
    )(h3, h3, h3, bias)


def _chunkattn_bias(rel_bias):
    j = np.arange(CA_HWIN)[:, None]
    r = np.arange(CA_HALF)[None, :]
    qc, kc = r // CHUNK, j // CHUNK
    band = (kc >= qc) & (kc <= qc + CA_LEFT_CHUNKS)
    n_diag = CA_HWIN + CA_HALF
    u = np.arange(n_diag)
    d = np.where(u < CA_HALF, u, u - n_diag)
    idx = np.clip(d + CA_PAD, -REL_CLIP, REL_CLIP) + REL_CLIP
    diag = rel_bias[:, idx].astype(F32) * LOG2E
    shifted = jnp.tile(diag, (1, CA_HWIN))[:, :CA_HWIN * (n_diag - 1)].reshape(CA_HEADS, CA_HWIN, n_diag - 1)
    table = shifted[:, :, :CA_HALF]
    return jnp.where(jnp.asarray(band)[None], table, NEG_INF)


def _diffattn_kernel(q_ref, k_ref, v_ref, lq1_ref, lk1_ref, lq2_ref, lk2_ref, linit_ref, ng_ref, o_ref,
                     vt_ref, s_ref, pv_ref):
    s_len = q_ref.shape[0]
    n_blocks = s_len // DA_BLOCK
    hw = 2 * DA_DH
    lam = (jnp.exp(jnp.sum(lq1_ref[...] * lk1_ref[...], axis=-1, keepdims=True))
           - jnp.exp(jnp.sum(lq2_ref[...] * lk2_ref[...], axis=-1, keepdims=True)) + linit_ref[...])
    out_scale = ng_ref[...] * (1.0 - linit_ref[...])
    for c in range(n_blocks):
        blk = slice(c * DA_BLOCK, (c + 1) * DA_BLOCK)
        vt = v_ref[blk, :].astype(F32).T.astype(BF16)
        for pr in range(DA_PAIRS):
            vt_ref[pr, 0:hw, blk] = vt[pr * hw:(pr + 1) * hw, :]
    vt_ref[:, hw:, :] = jnp.ones((DA_PAIRS, ONES_ROWS, s_len), BF16)
    key_chunk = lax.broadcasted_iota(jnp.int32, (DA_BLOCK, DA_BLOCK), 0) // CHUNK
    query_chunk = lax.broadcasted_iota(jnp.int32, (DA_BLOCK, DA_BLOCK), 1) // CHUNK
    diag_allowed = key_chunk <= query_chunk
    lane = lax.broadcasted_iota(jnp.int32, (DA_BLOCK, hw), 1)
    head_lanes = (lane < DA_DH, lane >= DA_DH)

    steps = [(i, t, c, pr) for i in range(n_blocks) for t in range(2) for c in range(i + 1) for pr in range(DA_PAIRS)]
    maxima = {}

    def scores(n):
        i, t, c, pr = steps[n]
        lanes = slice(pr * hw, (pr + 1) * hw)
        q = q_ref[i * DA_BLOCK:(i + 1) * DA_BLOCK, lanes]
        qz = jnp.where(head_lanes[t], q, jnp.zeros_like(q))
        s = _dot_nt(k_ref[c * DA_BLOCK:(c + 1) * DA_BLOCK, lanes], qz)
        if c == i:
            s = jnp.where(diag_allowed, s, NEG_INF)
        s_ref[n % DA_RING] = s
        maxima[n] = jnp.max(s, axis=0, keepdims=True)

    def attend(n):
        i, t, c, pr = steps[n]
        p = jnp.exp2(s_ref[n % DA_RING] - maxima[n]).astype(BF16)
        pv_ref[pr, i % 2, t, c] = _dot(vt_ref[pr, :, c * DA_BLOCK:(c + 1) * DA_BLOCK], p)
        if c == i:
            mine = [n - (i - j) * DA_PAIRS for j in range(i + 1)]
            m = functools.reduce(jnp.maximum, [maxima[j] for j in mine])
            acc = None
            for j, nj in enumerate(mine):
                part = pv_ref[pr, i % 2, t, j] * jnp.exp2(maxima.pop(nj) - m)
                acc = part if acc is None else acc + part
            heads[pr][t] = acc[0:hw] * (1.0 / acc[hw:hw + 1])
            if t == 1:
                o_t = heads[pr][0] - lam * heads[pr][1]
                ms = jnp.mean(o_t * o_t, axis=0, keepdims=True)
                y = (o_t * lax.rsqrt(ms + LN_EPS)).T * out_scale
                o_ref[i * DA_BLOCK:(i + 1) * DA_BLOCK, pr * hw:(pr + 1) * hw] = y.astype(BF16)

    heads = [[None, None] for _ in range(DA_PAIRS)]
    for n in range(len(steps) + DA_AHEAD):
        if n < len(steps):
            scores(n)
        if n >= DA_AHEAD:
            attend(n - DA_AHEAD)


def _diffattn(h3, lams, norm_g, lambda_init):
    b, s, _ = h3.shape
    hw = 2 * DA_DH
    bw = DA_PAIRS * hw
    small = pl.BlockSpec((1, DA_DH), lambda i, h: (0, 0))
    return pl.pallas_call(
        _diffattn_kernel,
        grid=(b, DA_HEADS // DA_PAIRS),
        in_specs=[pl.BlockSpec((None, s, bw), lambda i, h: (i, 0, OFF_DQ // bw + h)),
                  pl.BlockSpec((None, s, bw), lambda i, h: (i, 0, OFF_DK // bw + h)),
                  pl.BlockSpec((None, s, bw), lambda i, h: (i, 0, OFF_DV // bw + h)),
                  small, small, small, small,
                  pl.BlockSpec((1, 1), lambda i, h: (0, 0)),
                  pl.BlockSpec((1, hw), lambda i, h: (0, 0))],
        out_specs=pl.BlockSpec((None, s, bw), lambda i, h: (i, 0, h)),
        out_shape=jax.ShapeDtypeStruct((b, s, DA_V_W), BF16),
        scratch_shapes=[pltpu.VMEM((DA_PAIRS, hw + ONES_ROWS, s), BF16),
                        pltpu.VMEM((DA_RING, DA_BLOCK, DA_BLOCK), F32),
                        pltpu.VMEM((DA_PAIRS, 2, 2, s // DA_BLOCK, hw + ONES_ROWS, DA_BLOCK), F32)],
        compiler_params=_params(2),
        name="diffattn",
    )(h3, h3, h3, *lams, jnp.full((1, 1), lambda_init, F32), norm_g)


def _layer_norm(y, g, b):
    mu = jnp.mean(y, axis=-1, keepdims=True)
    d = y - mu
    var = jnp.mean(d * d, axis=-1, keepdims=True)
    return d * lax.rsqrt(var + LN_EPS) * g + b


def _merge_kernel(layer_ref, x_ref, oa_ref, ob_ref, oc_ref, wg_ref, bg_ref, wa_ref, wb_ref, wc_ref, wo_ref, g_ref, b_ref,
                  o_ref):
    for r0 in range(0, x_ref.shape[0], SUB_ROWS):
        rows = slice(r0, r0 + SUB_ROWS)
        xf = x_ref[rows, :]
        xb = xf.astype(BF16)
        merged = None
        for n, (br_ref, w_ref) in enumerate(((oa_ref, wa_ref), (ob_ref, wb_ref), (oc_ref, wc_ref))):
            cols = slice(n * D_MODEL, (n + 1) * D_MODEL)
            z = _dot(xb, wg_ref[:, cols]) + bg_ref[:, cols]
            term = _dot(br_ref[rows, :], w_ref[...]) * (1.0 / (1.0 + jnp.exp(-z)))
            merged = term if merged is None else merged + term
        mix = _dot(merged.astype(BF16), wo_ref[...])
        o_ref[rows, :] = _layer_norm(DEEPNORM_ALPHA * xf + mix, g_ref[...], b_ref[...])


def _merge(layer, x2, oa, ob, oc, w_in, bg, wa, wb, wc, wo, g, b, tm):
    t = x2.shape[0]
    row = lambda w: pl.BlockSpec((tm, w), lambda i, l: (i, 0))
    return pl.pallas_call(
        _merge_kernel,
        grid_spec=pltpu.PrefetchScalarGridSpec(
            num_scalar_prefetch=1,
            grid=(t // tm,),
            in_specs=[row(D_MODEL), row(RET_V_W), row(CA_W), row(DA_V_W),
                      pl.BlockSpec((None, D_MODEL, GATE_W), lambda i, l: (l[0], 0, MIX_W // GATE_W),
                                   pipeline_mode=pl.Buffered(1)),
                      _const_spec((1, GATE_W)),
                      _const_spec((RET_V_W, D_MODEL)), _const_spec((CA_W, D_MODEL)), _const_spec((DA_V_W, D_MODEL)),
                      _const_spec((D_MODEL, D_MODEL)), _const_spec((1, D_MODEL)), _const_spec((1, D_MODEL))],
            out_specs=row(D_MODEL)),
        out_shape=jax.ShapeDtypeStruct((t, D_MODEL), F32),
        compiler_params=_params(1),
        name="merge",
    )(layer, x2, oa, ob, oc, w_in, bg, wa, wb, wc, wo, g, b)


def _ffn_kernel(x_ref, wi_ref, wo_ref, g_ref, b_ref, o_ref, act_ref):
    for r0 in range(0, x_ref.shape[0], SUB_ROWS):
        rows = slice(r0, r0 + SUB_ROWS)
        xf = x_ref[rows, :]
        xb = xf.astype(BF16)
        for c in range(0, FFN_HIDDEN, FFN_CHUNK):
            ug = _dot(xb, wi_ref[:, c:c + FFN_CHUNK])
            uu = _dot(xb, wi_ref[:, FFN_HIDDEN + c:FFN_HIDDEN + c + FFN_CHUNK])
            act_ref[rows, c:c + FFN_CHUNK] = (ug / (1.0 + jnp.exp(-ug)) * uu).astype(BF16)
        ffn = _dot(act_ref[rows, :], wo_ref[...])
        o_ref[rows, :] = _layer_norm(DEEPNORM_ALPHA * xf + ffn, g_ref[...], b_ref[...])


def _ffn(x2, wi, wo, g, b, tm):
    t = x2.shape[0]
    return pl.pallas_call(
        _ffn_kernel,
        grid=(t // tm,),
        in_specs=[pl.BlockSpec((tm, D_MODEL), lambda i: (i, 0)),
                  _const_spec((D_MODEL, 2 * FFN_HIDDEN)), _const_spec((FFN_HIDDEN, D_MODEL)),
                  _const_spec((1, D_MODEL)), _const_spec((1, D_MODEL))],
        out_specs=pl.BlockSpec((tm, D_MODEL), lambda i: (i, 0)),
        out_shape=jax.ShapeDtypeStruct((t, D_MODEL), F32),
        scratch_shapes=[pltpu.VMEM((tm, FFN_HIDDEN), BF16)],
        compiler_params=_params(1),
        name="ffn",
    )(x2, wi, wo, g, b)


def _rotary_tables(seq):
    pos = jnp.arange(seq, dtype=F32)[:, None]

    def cs(d):
        inv_freq = ROPE_THETA ** (-jnp.arange(0, d, 2, dtype=F32) / d)
        ang = pos * inv_freq[None, :]
        return jnp.cos(ang), jnp.sin(ang)

    rc, rs = cs(RET_DK)
    rcos = jnp.concatenate([rc, rc], axis=1)
    rsin = jnp.concatenate([-rs, rs], axis=1)
    dc, ds = cs(DA_DH)
    z = jnp.zeros_like(ds)
    dcos = jnp.concatenate([dc, dc, dc, dc], axis=1)
    dsina = jnp.concatenate([-ds, z, -ds, z], axis=1)
    dsinb = jnp.concatenate([z, ds, z, ds], axis=1)
    return rcos, rsin, dcos, dsina, dsinb


def kernel(x, w_in, ret_norm_g, ca_rel_bias, da_lambda_q1, da_lambda_k1, da_lambda_q2, da_lambda_k2, da_norm_g, w_branch_a, w_branch_b, w_branch_c, b_merge, w_out, ln1_g, ln1_b, w_ffn_in, w_ffn_out, ln2_g, ln2_b):
    b, s, d = x.shape
    t = b * s
    assert d == D_MODEL and w_in.shape == (DEPTH, D_MODEL, MIX_W + GATE_W)
    assert s % PROJ_ROWS == 0 and t % TAIL_ROWS == 0
    assert s % RET_BLOCK == 0 and s % DA_BLOCK == 0 and s % CA_QBLOCK == 0 and s >= CA_WIN
    assert MIX_W % GATE_W == 0 and DA_HEADS % DA_PAIRS == 0
    rot_tabs = _rotary_tables(s)
    ret_tabs = _retention_tables()
    row = lambda a: a.reshape(1, -1).astype(F32)

    w_in_b = w_in.astype(BF16)
    x2 = x.reshape(t, d)
    for l in range(DEPTH):
        lambda_init = 0.8 - 0.6 * math.exp(-0.3 * l)
        layer = jnp.full((1,), l, jnp.int32)
        h = _inproj(layer, x2, w_in_b, rot_tabs, s, PROJ_ROWS)
        h3 = h.reshape(b, s, MIX_W)
        o_a = _retention(h3, ret_tabs, row(ret_norm_g[l]))
        o_b = _chunkattn(h3, _chunkattn_bias(ca_rel_bias[l]))
        o_c = _diffattn(h3, (row(da_lambda_q1[l]), row(da_lambda_k1[l]), row(da_lambda_q2[l]), row(da_lambda_k2[l])),
                        row(da_norm_g[l]), lambda_init)
        x2 = _merge(layer, x2, o_a.reshape(t, RET_V_W), o_b.reshape(t, CA_W), o_c.reshape(t, DA_V_W),
                    w_in_b, row(b_merge[l]), w_branch_a[l].astype(BF16), w_branch_b[l].astype(BF16),
                    w_branch_c[l].astype(BF16), w_out[l].astype(BF16), row(ln1_g[l]), row(ln1_b[l]), TAIL_ROWS)
        x2 = _ffn(x2, w_ffn_in[l].astype(BF16), w_ffn_out[l].astype(BF16), row(ln2_g[l]), row(ln2_b[l]), TAIL_ROWS)
    return x2.reshape(b, s, d)
```

```python
import functools
import math

import jax
import jax.numpy as jnp
import numpy as np
from jax import lax
from jax.experimental import pallas as pl
from jax.experimental.pallas import tpu as pltpu

D_MODEL = 1024
DEPTH = 4
CHUNK = 64
RET_HEADS, RET_DK, RET_DV = 4, 128, 256
CA_HEADS, CA_DH, CA_LEFT_CHUNKS, REL_CLIP = 8, 64, 8, 256
DA_HEADS, DA_DH = 4, 64
FFN_HIDDEN = -(-8 * D_MODEL // (3 * 256)) * 256
ROPE_THETA = 10000.0
LN_EPS = 1e-5
NEG_INF = -1e30
DEEPNORM_ALPHA = (2.0 * DEPTH) ** 0.25

RET_QK_W = RET_HEADS * RET_DK
RET_V_W = RET_HEADS * RET_DV
CA_W = CA_HEADS * CA_DH
DA_QK_W = 2 * DA_HEADS * DA_DH
DA_V_W = DA_HEADS * 2 * DA_DH
OFF_RQ = 0
OFF_RK = OFF_RQ + RET_QK_W
OFF_RV = OFF_RK + RET_QK_W
OFF_RG = OFF_RV + RET_V_W
OFF_CQ = OFF_RG + RET_V_W
OFF_CK = OFF_CQ + CA_W
OFF_CV = OFF_CK + CA_W
OFF_DQ = OFF_CV + CA_W
OFF_DK = OFF_DQ + DA_QK_W
OFF_DV = OFF_DK + DA_QK_W
MIX_W = OFF_DV + DA_V_W
GATE_W = 3 * D_MODEL

LOG2E = math.log2(math.e)
LANES = 128
SUBLANES = 8
PROJ_COLS = 512
ONES_ROWS = 16
VMEM_LIMIT = 56 * 1024 * 1024

RET_BLOCK = 256
CA_QBLOCK = 256
CA_HALF = CA_QBLOCK // 2
CA_PAD = CA_LEFT_CHUNKS * CHUNK
CA_WIN = CA_PAD + CA_QBLOCK
CA_HWIN = CA_PAD + CA_HALF
CA_AHEAD = 2
CA_DEPTH = CA_AHEAD + 1
CA_SLAB = CA_DH + ONES_ROWS
DA_BLOCK = 256
DA_PAIRS = 2
DA_AHEAD = 6 * DA_PAIRS
DA_RING = DA_AHEAD + 1
SUB_ROWS = 256
PROJ_ROWS = 512
TAIL_ROWS = 4 * SUB_ROWS
FFN_CHUNK = 256

BF16 = jnp.bfloat16
F32 = jnp.float32


def _dot(a, b):
    return jnp.dot(a, b, preferred_element_type=F32)


def _dot_nt(a, b):
    return lax.dot_general(a, b, (((1,), (1,)), ((), ())), preferred_element_type=F32)


def _dot_tn(a, b):
    return lax.dot_general(a, b, (((0,), (0,)), ((), ())), preferred_element_type=F32)


def _const_spec(shape):
    zeros = (0,) * len(shape)
    return pl.BlockSpec(shape, lambda *_: zeros, pipeline_mode=pl.Buffered(1))


def _params(n_axes):
    return pltpu.CompilerParams(dimension_semantics=("arbitrary",) * n_axes, vmem_limit_bytes=VMEM_LIMIT)


def _inproj_kernel(layer_ref, x_ref, w_ref, rcos_ref, rsin_ref, dcos_ref, dsina_ref, dsinb_ref, o_ref):
    xb = x_ref[...].astype(BF16)

    def proj(off, width):
        return _dot(xb, w_ref[:, off:off + width])

    def ret_rotary(a, scale):
        outs = []
        for h in range(RET_HEADS):
            ah = a[:, h * RET_DK:(h + 1) * RET_DK]
            r = ah * rcos_ref[...] + pltpu.roll(ah, RET_DK // 2, 1) * rsin_ref[...]
            outs.append(r * scale if scale != 1.0 else r)
        return jnp.concatenate(outs, axis=1)

    def da_rotary(a, scale):
        outs = []
        for g in range(DA_QK_W // LANES):
            ag = a[:, g * LANES:(g + 1) * LANES]
            r = (ag * dcos_ref[...] + pltpu.roll(ag, LANES - DA_DH // 2, 1) * dsina_ref[...]
                 + pltpu.roll(ag, DA_DH // 2, 1) * dsinb_ref[...])
            outs.append(r * scale if scale != 1.0 else r)
        return jnp.concatenate(outs, axis=1)

    o_ref[:, OFF_RQ:OFF_RQ + RET_QK_W] = ret_rotary(proj(OFF_RQ, RET_QK_W), 1.0).astype(BF16)
    o_ref[:, OFF_RK:OFF_RK + RET_QK_W] = ret_rotary(proj(OFF_RK, RET_QK_W), RET_DK ** -0.5).astype(BF16)
    for off in range(OFF_RV, OFF_CQ, PROJ_COLS):
        o_ref[:, off:off + PROJ_COLS] = proj(off, PROJ_COLS).astype(BF16)
    o_ref[:, OFF_CQ:OFF_CQ + CA_W] = (proj(OFF_CQ, CA_W) * (CA_DH ** -0.5 * LOG2E)).astype(BF16)
    o_ref[:, OFF_CK:OFF_CK + CA_W] = proj(OFF_CK, CA_W).astype(BF16)
    o_ref[:, OFF_CV:OFF_CV + CA_W] = proj(OFF_CV, CA_W).astype(BF16)
    o_ref[:, OFF_DQ:OFF_DQ + DA_QK_W] = da_rotary(proj(OFF_DQ, DA_QK_W), DA_DH ** -0.5 * LOG2E).astype(BF16)
    o_ref[:, OFF_DK:OFF_DK + DA_QK_W] = da_rotary(proj(OFF_DK, DA_QK_W), 1.0).astype(BF16)
    o_ref[:, OFF_DV:OFF_DV + DA_V_W] = proj(OFF_DV, DA_V_W).astype(BF16)


def _inproj(layer, x2, w_in, tabs, seq, tm):
    t = x2.shape[0]
    pos_blocks = seq // tm
    tab_spec = pl.BlockSpec((tm, LANES), lambda i, l: (i % pos_blocks, 0))
    return pl.pallas_call(
        _inproj_kernel,
        grid_spec=pltpu.PrefetchScalarGridSpec(
            num_scalar_prefetch=1,
            grid=(t // tm,),
            in_specs=[pl.BlockSpec((tm, D_MODEL), lambda i, l: (i, 0)),
                      pl.BlockSpec((None, D_MODEL, MIX_W), lambda i, l: (l[0], 0, 0), pipeline_mode=pl.Buffered(1)),
                      tab_spec, tab_spec, tab_spec, tab_spec, tab_spec],
            out_specs=pl.BlockSpec((tm, MIX_W), lambda i, l: (i, 0))),
        out_shape=jax.ShapeDtypeStruct((t, MIX_W), BF16),
        compiler_params=_params(1),
        name="inproj",
    )(layer, x2, w_in, *tabs)


def _retention_kernel(q_ref, k_ref, v_ref, g_ref, dmat_ref, qdec_ref, kdec_ref, cdec_ref, ng_ref, o_ref,
                      state_ref, sv_ref, kv_ref):
    n_blocks = q_ref.shape[0] // RET_BLOCK

    def local(h, c):
        rows = slice(c * RET_BLOCK, (c + 1) * RET_BLOCK)
        q = q_ref[rows, h * RET_DK:(h + 1) * RET_DK]
        k = k_ref[rows, h * RET_DK:(h + 1) * RET_DK]
        v = v_ref[rows, h * RET_DV:(h + 1) * RET_DV]
        s = _dot_nt(q, k) * dmat_ref[h]
        sv_ref[c % 2, h] = _dot(s.astype(BF16), v)
        kd = (k.astype(F32) * kdec_ref[h]).astype(BF16)
        kv_ref[c % 2, h] = _dot_tn(kd, v)

    def finish(h, c):
        rows = slice(c * RET_BLOCK, (c + 1) * RET_BLOCK)
        cols = slice(h * RET_DV, (h + 1) * RET_DV)
        o = sv_ref[c % 2, h]
        if c > 0:
            qd = (q_ref[rows, h * RET_DK:(h + 1) * RET_DK].astype(F32) * qdec_ref[h]).astype(BF16)
            o = o + _dot(qd, state_ref[h].astype(BF16))
            state_ref[h] = state_ref[h] * cdec_ref[h, 0:1, :] + kv_ref[c % 2, h]
        else:
            state_ref[h] = kv_ref[c % 2, h]
        ms = jnp.mean(o * o, axis=-1, keepdims=True)
        y = o * lax.rsqrt(ms + LN_EPS) * ng_ref[...]
        half = 0.5 * g_ref[rows, cols].astype(F32)
        silu = half + half * jnp.tanh(half)
        o_ref[rows, cols] = (silu * y).astype(BF16)

    for h in range(RET_HEADS):
        local(h, 0)
    for c in range(n_blocks):
        for h in range(RET_HEADS):
            if c + 1 < n_blocks:
                local(h, c + 1)
            finish(h, c)


def _retention(h3, tabs, norm_g):
    b, s, _ = h3.shape
    dmat, qdec, kdec, cdec = tabs
    return pl.pallas_call(
        _retention_kernel,
        grid=(b,),
        in_specs=[pl.BlockSpec((None, s, RET_QK_W), lambda i: (i, 0, OFF_RQ // RET_QK_W)),
                  pl.BlockSpec((None, s, RET_QK_W), lambda i: (i, 0, OFF_RK // RET_QK_W)),
                  pl.BlockSpec((None, s, RET_V_W), lambda i: (i, 0, OFF_RV // RET_V_W)),
                  pl.BlockSpec((None, s, RET_V_W), lambda i: (i, 0, OFF_RG // RET_V_W)),
                  _const_spec((RET_HEADS, RET_BLOCK, RET_BLOCK)),
                  _const_spec((RET_HEADS, RET_BLOCK, RET_DK)),
                  _const_spec((RET_HEADS, RET_BLOCK, RET_DK)),
                  _const_spec((RET_HEADS, SUBLANES, RET_DV)),
                  _const_spec((1, RET_DV))],
        out_specs=pl.BlockSpec((None, s, RET_V_W), lambda i: (i, 0, 0)),
        out_shape=jax.ShapeDtypeStruct((b, s, RET_V_W), BF16),
        scratch_shapes=[pltpu.VMEM((RET_HEADS, RET_DK, RET_DV), F32),
                        pltpu.VMEM((2, RET_HEADS, RET_BLOCK, RET_DV), F32),
                        pltpu.VMEM((2, RET_HEADS, RET_DK, RET_DV), F32)],
        compiler_params=_params(1),
        name="retention",
    )(h3, h3, h3, h3, dmat, qdec, kdec, cdec, norm_g)


def _retention_tables():
    h = np.arange(RET_HEADS, dtype=np.float64)
    log_gamma = np.log1p(-np.exp2(-5.0 - h))
    n = np.arange(RET_BLOCK)
    diff = (n[:, None] - n[None, :]).astype(np.float64)
    cn, cm = n[:, None] // CHUNK, n[None, :] // CHUNK
    expo = np.where(cm == cn, np.abs(diff), diff)
    dmat = np.where(cm <= cn, np.exp(log_gamma[:, None, None] * expo[None]), 0.0)
    qdec = np.exp(log_gamma[:, None] * (n[None, :] + 1.0))
    kdec = np.exp(log_gamma[:, None] * (RET_BLOCK - 1.0 - n[None, :]))
    cdec = np.exp(log_gamma * RET_BLOCK)
    qdec = np.broadcast_to(qdec[:, :, None], (RET_HEADS, RET_BLOCK, RET_DK))
    kdec = np.broadcast_to(kdec[:, :, None], (RET_HEADS, RET_BLOCK, RET_DK))
    cdec = np.broadcast_to(cdec[:, None, None], (RET_HEADS, SUBLANES, RET_DV))
    return tuple(jnp.asarray(a, F32) for a in (dmat, qdec, kdec, cdec))


def _chunkattn_kernel(q_ref, k_ref, v_ref, bias_ref, o_ref, kpad_ref, vt_ref, s_ref, p_ref, ot_ref):
    s_len = q_ref.shape[0]
    n_blocks = s_len // CA_QBLOCK
    n_pad = CA_PAD // CA_QBLOCK
    kpad_ref[0:CA_PAD, :] = jnp.zeros((CA_PAD, CA_W), BF16)
    kpad_ref[CA_PAD:, :] = k_ref[...]
    ones = jnp.ones((ONES_ROWS, CA_QBLOCK), BF16)
    for c in range(n_pad + n_blocks):
        if c < n_pad:
            vt = jnp.zeros((CA_W, CA_QBLOCK), BF16)
        else:
            vt = v_ref[(c - n_pad) * CA_QBLOCK:(c - n_pad + 1) * CA_QBLOCK, :].astype(F32).T.astype(BF16)
        for h in range(CA_HEADS):
            vt_ref[c, h * CA_SLAB:h * CA_SLAB + CA_DH, :] = vt[h * CA_DH:(h + 1) * CA_DH, :]
            vt_ref[c, h * CA_SLAB + CA_DH:(h + 1) * CA_SLAB, :] = ones
    p_ref[:, CA_HWIN:CA_WIN, 0:CA_HALF] = jnp.zeros((2, CA_WIN - CA_HWIN, CA_HALF), BF16)
    p_ref[:, 0:CA_HALF, CA_HALF:CA_QBLOCK] = jnp.zeros((2, CA_HALF, CA_HALF), BF16)
    lane = lax.broadcasted_iota(jnp.int32, (CA_QBLOCK, LANES), 1)
    head_lanes = (lane < CA_DH, lane >= CA_DH)
    win_row = lax.broadcasted_iota(jnp.int32, (CA_HWIN, CA_HALF), 0)

    def scores(n):
        i, h = divmod(n, CA_HEADS)
        pair = slice((h // 2) * LANES, (h // 2 + 1) * LANES)
        q = q_ref[i * CA_QBLOCK:(i + 1) * CA_QBLOCK, pair]
        qz = jnp.where(head_lanes[h % 2], q, jnp.zeros_like(q))
        s_ref[n % CA_DEPTH] = _dot_nt(kpad_ref[i * CA_QBLOCK:i * CA_QBLOCK + CA_WIN, pair], qz)

    def attend(n):
        i, h = divmod(n, CA_HEADS)
        first_key = i * CA_QBLOCK - CA_PAD
        for half in range(2):
            rows = slice(half * CA_HALF, half * CA_HALF + CA_HWIN)
            cols = slice(half * CA_HALF, (half + 1) * CA_HALF)
            s = s_ref[n % CA_DEPTH, rows, cols] + bias_ref[h]
            if first_key + half * CA_HALF < 0:
                s = jnp.where(win_row >= -(first_key + half * CA_HALF), s, NEG_INF)
            m = jnp.max(s, axis=0, keepdims=True)
            p_ref[h % 2, rows, cols] = jnp.exp2(s - m).astype(BF16)
        pv = None
        for c in range(CA_WIN // CA_QBLOCK):
            part = _dot(vt_ref[i + c, h * CA_SLAB:(h + 1) * CA_SLAB, :],
                        p_ref[h % 2, c * CA_QBLOCK:(c + 1) * CA_QBLOCK, :])
            pv = part if pv is None else pv + part
        ot_ref[h * CA_DH:(h + 1) * CA_DH, :] = pv[0:CA_DH] * (1.0 / pv[CA_DH:CA_DH + 1])
        if h == CA_HEADS - 1:
            o_ref[i * CA_QBLOCK:(i + 1) * CA_QBLOCK, :] = ot_ref[...].T.astype(BF16)

    n_steps = n_blocks * CA_HEADS
    for n in range(n_steps + CA_AHEAD):
        if n < n_steps:
            scores(n)
        if n >= CA_AHEAD:
            attend(n - CA_AHEAD)


def _chunkattn(h3, bias):
    b, s, _ = h3.shape
    n_blocks = s // CA_QBLOCK
    return pl.pallas_call(
        _chunkattn_kernel,
        grid=(b,),
        in_specs=[pl.BlockSpec((None, s, CA_W), lambda i: (i, 0, OFF_CQ // CA_W)),
                  pl.BlockSpec((None, s, CA_W), lambda i: (i, 0, OFF_CK // CA_W)),
                  pl.BlockSpec((None, s, CA_W), lambda i: (i, 0, OFF_CV // CA_W)),
                  _const_spec((CA_HEADS, CA_HWIN, CA_HALF))],
        out_specs=pl.BlockSpec((None, s, CA_W), lambda i: (i, 0, 0)),
        out_shape=jax.ShapeDtypeStruct((b, s, CA_W), BF16),
        scratch_shapes=[pltpu.VMEM((s + CA_PAD, CA_W), BF16),
                        pltpu.VMEM((n_blocks + CA_PAD // CA_QBLOCK, CA_HEADS * CA_SLAB, CA_QBLOCK), BF16),
                        pltpu.VMEM((CA_DEPTH, CA_WIN, CA_QBLOCK), F32),
                        pltpu.VMEM((2, CA_WIN, CA_QBLOCK), BF16),
                        pltpu.VMEM((CA_W, CA_QBLOCK), F32)],
        compiler_params=_params(1),
        name="chunkattn",
    )(h3, h3, h3, bias)


def _chunkattn_bias(rel_bias):
    j = np.arange(CA_HWIN)[:, None]
    r = np.arange(CA_HALF)[None, :]
    qc, kc = r // CHUNK, j // CHUNK
    band = (kc >= qc) & (kc <= qc + CA_LEFT_CHUNKS)
    n_diag = CA_HWIN + CA_HALF
    u = np.arange(n_diag)
    d = np.where(u < CA_HALF, u, u - n_diag)
    idx = np.clip(d + CA_PAD, -REL_CLIP, REL_CLIP) + REL_CLIP
    diag = rel_bias[:, idx].astype(F32) * LOG2E
    shifted = jnp.tile(diag, (1, CA_HWIN))[:, :CA_HWIN * (n_diag - 1)].reshape(CA_HEADS, CA_HWIN, n_diag - 1)
    table = shifted[:, :, :CA_HALF]
    return jnp.where(jnp.asarray(band)[None], table, NEG_INF)


def _diffattn_kernel(q_ref, k_ref, v_ref, lq1_ref, lk1_ref, lq2_ref, lk2_ref, linit_ref, ng_ref, o_ref,
                     vt_ref, s_ref, pv_ref):
    s_len = q_ref.shape[0]
    n_blocks = s_len // DA_BLOCK
    hw = 2 * DA_DH
    lam = (jnp.exp(jnp.sum(lq1_ref[...] * lk1_ref[...], axis=-1, keepdims=True))
           - jnp.exp(jnp.sum(lq2_ref[...] * lk2_ref[...], axis=-1, keepdims=True)) + linit_ref[...])
    out_scale = ng_ref[...] * (1.0 - linit_ref[...])
    for c in range(n_blocks):
        blk = slice(c * DA_BLOCK, (c + 1) * DA_BLOCK)
        vt = v_ref[blk, :].astype(F32).T.astype(BF16)
        for pr in range(DA_PAIRS):
            vt_ref[pr, 0:hw, blk] = vt[pr * hw:(pr + 1) * hw, :]
    vt_ref[:, hw:, :] = jnp.ones((DA_PAIRS, ONES_ROWS, s_len), BF16)
    key_chunk = lax.broadcasted_iota(jnp.int32, (DA_BLOCK, DA_BLOCK), 0) // CHUNK
    query_chunk = lax.broadcasted_iota(jnp.int32, (DA_BLOCK, DA_BLOCK), 1) // CHUNK
    diag_allowed = key_chunk <= query_chunk
    lane = lax.broadcasted_iota(jnp.int32, (DA_BLOCK, hw), 1)
    head_lanes = (lane < DA_DH, lane >= DA_DH)

    steps = [(i, t, c, pr) for i in range(n_blocks) for t in range(2) for c in range(i + 1) for pr in range(DA_PAIRS)]
    maxima = {}
    masked_q = {}

    def scores(n):
        i, t, c, pr = steps[n]
        lanes = slice(pr * hw, (pr + 1) * hw)
        if c == 0:
            q = q_ref[i * DA_BLOCK:(i + 1) * DA_BLOCK, lanes]
            masked_q[pr] = jnp.where(head_lanes[t], q, jnp.zeros_like(q))
        s = _dot_nt(k_ref[c * DA_BLOCK:(c + 1) * DA_BLOCK, lanes], masked_q[pr])
        if c == i:
            s = jnp.where(diag_allowed, s, NEG_INF)
        s_ref[n % DA_RING] = s
        maxima[n] = jnp.max(s, axis=0, keepdims=True)

    def attend(n):
        i, t, c, pr = steps[n]
        p = jnp.exp2(s_ref[n % DA_RING] - maxima[n]).astype(BF16)
        pv_ref[pr, i % 2, t, c] = _dot(vt_ref[pr, :, c * DA_BLOCK:(c + 1) * DA_BLOCK], p)
        if c == i:
            mine = [n - (i - j) * DA_PAIRS for j in range(i + 1)]
            m = functools.reduce(jnp.maximum, [maxima[j] for j in mine])
            acc = None
            for j, nj in enumerate(mine):
                part = pv_ref[pr, i % 2, t, j] * jnp.exp2(maxima.pop(nj) - m)
                acc = part if acc is None else acc + part
            heads[pr][t] = acc[0:hw] * (1.0 / acc[hw:hw + 1])
            if t == 1:
                o_t = heads[pr][0] - lam * heads[pr][1]
                ms = jnp.mean(o_t * o_t, axis=0, keepdims=True)
                y = (o_t * lax.rsqrt(ms + LN_EPS)).T * out_scale
                o_ref[i * DA_BLOCK:(i + 1) * DA_BLOCK, pr * hw:(pr + 1) * hw] = y.astype(BF16)

    heads = [[None, None] for _ in range(DA_PAIRS)]
    for n in range(len(steps) + DA_AHEAD):
        if n < len(steps):
            scores(n)
        if n >= DA_AHEAD:
            attend(n - DA_AHEAD)


def _diffattn(h3, lams, norm_g, lambda_init):
    b, s, _ = h3.shape
    hw = 2 * DA_DH
    bw = DA_PAIRS * hw
    small = pl.BlockSpec((1, DA_DH), lambda i, h: (0, 0))
    return pl.pallas_call(
        _diffattn_kernel,
        grid=(b, DA_HEADS // DA_PAIRS),
        in_specs=[pl.BlockSpec((None, s, bw), lambda i, h: (i, 0, OFF_DQ // bw + h)),
                  pl.BlockSpec((None, s, bw), lambda i, h: (i, 0, OFF_DK // bw + h)),
                  pl.BlockSpec((None, s, bw), lambda i, h: (i, 0, OFF_DV // bw + h)),
                  small, small, small, small,
                  pl.BlockSpec((1, 1), lambda i, h: (0, 0)),
                  pl.BlockSpec((1, hw), lambda i, h: (0, 0))],
        out_specs=pl.BlockSpec((None, s, bw), lambda i, h: (i, 0, h)),
        out_shape=jax.ShapeDtypeStruct((b, s, DA_V_W), BF16),
        scratch_shapes=[pltpu.VMEM((DA_PAIRS, hw + ONES_ROWS, s), BF16),
                        pltpu.VMEM((DA_RING, DA_BLOCK, DA_BLOCK), F32),
                        pltpu.VMEM((DA_PAIRS, 2, 2, s // DA_BLOCK, hw + ONES_ROWS, DA_BLOCK), F32)],
        compiler_params=_params(2),
        name="diffattn",
    )(h3, h3, h3, *lams, jnp.full((1, 1), lambda_init, F32), norm_g)


def _layer_norm(y, g, b):
    mu = jnp.mean(y, axis=-1, keepdims=True)
    d = y - mu
    var = jnp.mean(d * d, axis=-1, keepdims=True)
    return d * lax.rsqrt(var + LN_EPS) * g + b


def _merge_kernel(layer_ref, x_ref, oa_ref, ob_ref, oc_ref, wg_ref, bg_ref, wa_ref, wb_ref, wc_ref, wo_ref, g_ref, b_ref,
                  o_ref):
    for r0 in range(0, x_ref.shape[0], SUB_ROWS):
        rows = slice(r0, r0 + SUB_ROWS)
        xf = x_ref[rows, :]
        xb = xf.astype(BF16)
        merged = None
        for n, (br_ref, w_ref) in enumerate(((oa_ref, wa_ref), (ob_ref, wb_ref), (oc_ref, wc_ref))):
            cols = slice(n * D_MODEL, (n + 1) * D_MODEL)
            z = _dot(xb, wg_ref[:, cols]) + bg_ref[:, cols]
            term = _dot(br_ref[rows, :], w_ref[...]) * (1.0 / (1.0 + jnp.exp(-z)))
            merged = term if merged is None else merged + term
        mix = _dot(merged.astype(BF16), wo_ref[...])
        o_ref[rows, :] = _layer_norm(DEEPNORM_ALPHA * xf + mix, g_ref[...], b_ref[...])


def _merge(layer, x2, oa, ob, oc, w_in, bg, wa, wb, wc, wo, g, b, tm):
    t = x2.shape[0]
    row = lambda w: pl.BlockSpec((tm, w), lambda i, l: (i, 0))
    return pl.pallas_call(
        _merge_kernel,
        grid_spec=pltpu.PrefetchScalarGridSpec(
            num_scalar_prefetch=1,
            grid=(t // tm,),
            in_specs=[row(D_MODEL), row(RET_V_W), row(CA_W), row(DA_V_W),
                      pl.BlockSpec((None, D_MODEL, GATE_W), lambda i, l: (l[0], 0, MIX_W // GATE_W),
                                   pipeline_mode=pl.Buffered(1)),
                      _const_spec((1, GATE_W)),
                      _const_spec((RET_V_W, D_MODEL)), _const_spec((CA_W, D_MODEL)), _const_spec((DA_V_W, D_MODEL)),
                      _const_spec((D_MODEL, D_MODEL)), _const_spec((1, D_MODEL)), _const_spec((1, D_MODEL))],
            out_specs=row(D_MODEL)),
        out_shape=jax.ShapeDtypeStruct((t, D_MODEL), F32),
        compiler_params=_params(1),
        name="merge",
    )(layer, x2, oa, ob, oc, w_in, bg, wa, wb, wc, wo, g, b)


def _ffn_kernel(x_ref, wi_ref, wo_ref, g_ref, b_ref, o_ref, act_ref):
    for r0 in range(0, x_ref.shape[0], SUB_ROWS):
        rows = slice(r0, r0 + SUB_ROWS)
        xf = x_ref[rows, :]
        xb = xf.astype(BF16)
        for c in range(0, FFN_HIDDEN, FFN_CHUNK):
            ug = _dot(xb, wi_ref[:, c:c + FFN_CHUNK])
            uu = _dot(xb, wi_ref[:, FFN_HIDDEN + c:FFN_HIDDEN + c + FFN_CHUNK])
            half = 0.5 * ug
            silu = half + half * jnp.tanh(half)
            act_ref[rows, c:c + FFN_CHUNK] = (silu * uu).astype(BF16)
        ffn = _dot(act_ref[rows, :], wo_ref[...])
        o_ref[rows, :] = _layer_norm(DEEPNORM_ALPHA * xf + ffn, g_ref[...], b_ref[...])


def _ffn(x2, wi, wo, g, b, tm):
    t = x2.shape[0]
    return pl.pallas_call(
        _ffn_kernel,
        grid=(t // tm,),
        in_specs=[pl.BlockSpec((tm, D_MODEL), lambda i: (i, 0)),
                  _const_spec((D_MODEL, 2 * FFN_HIDDEN)), _const_spec((FFN_HIDDEN, D_MODEL)),
                  _const_spec((1, D_MODEL)), _const_spec((1, D_MODEL))],
        out_specs=pl.BlockSpec((tm, D_MODEL), lambda i: (i, 0)),
        out_shape=jax.ShapeDtypeStruct((t, D_MODEL), F32),
        scratch_shapes=[pltpu.VMEM((tm, FFN_HIDDEN), BF16)],
        compiler_params=_params(1),
        name="ffn",
    )(x2, wi, wo, g, b)


def _rotary_tables(seq):
    pos = jnp.arange(seq, dtype=F32)[:, None]

    def cs(d):
        inv_freq = ROPE_THETA ** (-jnp.arange(0, d, 2, dtype=F32) / d)
        ang = pos * inv_freq[None, :]
        return jnp.cos(ang), jnp.sin(ang)

    rc, rs = cs(RET_DK)
    rcos = jnp.concatenate([rc, rc], axis=1)
    rsin = jnp.concatenate([-rs, rs], axis=1)
    dc, ds = cs(DA_DH)
    z = jnp.zeros_like(ds)
    dcos = jnp.concatenate([dc, dc, dc, dc], axis=1)
    dsina = jnp.concatenate([-ds, z, -ds, z], axis=1)
    dsinb = jnp.concatenate([z, ds, z, ds], axis=1)
    return rcos, rsin, dcos, dsina, dsinb


def kernel(x, w_in, ret_norm_g, ca_rel_bias, da_lambda_q1, da_lambda_k1, da_lambda_q2, da_lambda_k2, da_norm_g, w_branch_a, w_branch_b, w_branch_c, b_merge, w_out, ln1_g, ln1_b, w_ffn_in, w_ffn_out, ln2_g, ln2_b):
    b, s, d = x.shape
    t = b * s
    assert d == D_MODEL and w_in.shape == (DEPTH, D_MODEL, MIX_W + GATE_W)
    assert s % PROJ_ROWS == 0 and t % TAIL_ROWS == 0
    assert s % RET_BLOCK == 0 and s % DA_BLOCK == 0 and s % CA_QBLOCK == 0 and s >= CA_WIN
    assert MIX_W % GATE_W == 0 and DA_HEADS % DA_PAIRS == 0
    rot_tabs = _rotary_tables(s)
    ret_tabs = _retention_tables()
    row = lambda a: a.reshape(1, -1).astype(F32)

    w_in_b = w_in.astype(BF16)
    x2 = x.reshape(t, d)
    for l in range(DEPTH):
        lambda_init = 0.8 - 0.6 * math.exp(-0.3 * l)
        layer = jnp.full((1,), l, jnp.int32)
        h = _inproj(layer, x2, w_in_b, rot_tabs, s, PROJ_ROWS)
        h3 = h.reshape(b, s, MIX_W)
        o_a = _retention(h3, ret_tabs, row(ret_norm_g[l]))
        o_b = _chunkattn(h3, _chunkattn_bias(ca_rel_bias[l]))
        o_c = _diffattn(h3, (row(da_lambda_q1[l]), row(da_lambda_k1[l]), row(da_lambda_q2[l]), row(da_lambda_k2[l])),
                        row(da_norm_g[l]), lambda_init)
        x2 = _merge(layer, x2, o_a.reshape(t, RET_V_W), o_b.reshape(t, CA_W), o_c.reshape(t, DA_V_W),
                    w_in_b, row(b_merge[l]), w_branch_a[l].astype(BF16), w_branch_b[l].astype(BF16),
                    w_branch_c[l].astype(BF16), w_out[l].astype(BF16), row(ln1_g[l]), row(ln1_b[l]), TAIL_ROWS)
        x2 = _ffn(x2, w_ffn_in[l].astype(BF16), w_ffn_out[l].astype(BF16), row(ln2_g[l]), row(ln2_b[l]), TAIL_ROWS)
    return x2.reshape(b, s, d)
```

```python
import functools
import math

import jax
import jax.numpy as jnp
import numpy as np
from jax import lax
from jax.experimental import pallas as pl
from jax.experimental.pallas import tpu as pltpu

D_MODEL = 1024
DEPTH = 4
CHUNK = 64
RET_HEADS, RET_DK, RET_DV = 4, 128, 256
CA_HEADS, CA_DH, CA_LEFT_CHUNKS, REL_CLIP = 8, 64, 8, 256
DA_HEADS, DA_DH = 4, 64
FFN_HIDDEN = -(-8 * D_MODEL // (3 * 256)) * 256
ROPE_THETA = 10000.0
LN_EPS = 1e-5
NEG_INF = -1e30
DEEPNORM_ALPHA = (2.0 * DEPTH) ** 0.25

RET_QK_W = RET_HEADS * RET_DK
RET_V_W = RET_HEADS * RET_DV
CA_W = CA_HEADS * CA_DH
DA_QK_W = 2 * DA_HEADS * DA_DH
DA_V_W = DA_HEADS * 2 * DA_DH
OFF_RQ = 0
OFF_RK = OFF_RQ + RET_QK_W
OFF_RV = OFF_RK + RET_QK_W
OFF_RG = OFF_RV + RET_V_W
OFF_CQ = OFF_RG + RET_V_W
OFF_CK = OFF_CQ + CA_W
OFF_CV = OFF_CK + CA_W
OFF_DQ = OFF_CV + CA_W
OFF_DK = OFF_DQ + DA_QK_W
OFF_DV = OFF_DK + DA_QK_W
MIX_W = OFF_DV + DA_V_W
GATE_W = 3 * D_MODEL

LOG2E = math.log2(math.e)
LANES = 128
SUBLANES = 8
PROJ_COLS = 512
ONES_ROWS = 16
VMEM_LIMIT = 56 * 1024 * 1024

RET_BLOCK = 256
CA_QBLOCK = 256
CA_HALF = CA_QBLOCK // 2
CA_PAD = CA_LEFT_CHUNKS * CHUNK
CA_WIN = CA_PAD + CA_QBLOCK
CA_HWIN = CA_PAD + CA_HALF
CA_AHEAD = 2
CA_DEPTH = CA_AHEAD + 1
CA_SLAB = CA_DH + ONES_ROWS
DA_BLOCK = 256
DA_PAIRS = 2
DA_AHEAD = 6 * DA_PAIRS
DA_RING = DA_AHEAD + 1
SUB_ROWS = 256
PROJ_ROWS = 512
TAIL_ROWS = 4 * SUB_ROWS
FFN_CHUNK = 256

BF16 = jnp.bfloat16
F32 = jnp.float32


def _dot(a, b):
    return jnp.dot(a, b, preferred_element_type=F32)


def _dot_nt(a, b):
    return lax.dot_general(a, b, (((1,), (1,)), ((), ())), preferred_element_type=F32)


def _dot_tn(a, b):
    return lax.dot_general(a, b, (((0,), (0,)), ((), ())), preferred_element_type=F32)


def _const_spec(shape):
    zeros = (0,) * len(shape)
    return pl.BlockSpec(shape, lambda *_: zeros, pipeline_mode=pl.Buffered(1))


def _params(n_axes):
    return pltpu.CompilerParams(dimension_semantics=("arbitrary",) * n_axes, vmem_limit_bytes=VMEM_LIMIT)


def _inproj_kernel(layer_ref, x_ref, w_ref, rcos_ref, rsin_ref, dcos_ref, dsina_ref, dsinb_ref, o_ref):
    xb = x_ref[...].astype(BF16)

    def proj(off, width):
        return _dot(xb, w_ref[:, off:off + width])

    def ret_rotary(a, scale):
        outs = []
        for h in range(RET_HEADS):
            ah = a[:, h * RET_DK:(h + 1) * RET_DK]
            r = ah * rcos_ref[...] + pltpu.roll(ah, RET_DK // 2, 1) * rsin_ref[...]
            outs.append(r * scale if scale != 1.0 else r)
        return jnp.concatenate(outs, axis=1)

    def da_rotary(a, scale):
        outs = []
        for g in range(DA_QK_W // LANES):
            ag = a[:, g * LANES:(g + 1) * LANES]
            r = (ag * dcos_ref[...] + pltpu.roll(ag, LANES - DA_DH // 2, 1) * dsina_ref[...]
                 + pltpu.roll(ag, DA_DH // 2, 1) * dsinb_ref[...])
            outs.append(r * scale if scale != 1.0 else r)
        return jnp.concatenate(outs, axis=1)

    o_ref[:, OFF_RQ:OFF_RQ + RET_QK_W] = ret_rotary(proj(OFF_RQ, RET_QK_W), 1.0).astype(BF16)
    o_ref[:, OFF_RK:OFF_RK + RET_QK_W] = ret_rotary(proj(OFF_RK, RET_QK_W), RET_DK ** -0.5).astype(BF16)
    for off in range(OFF_RV, OFF_CQ, PROJ_COLS):
        o_ref[:, off:off + PROJ_COLS] = proj(off, PROJ_COLS).astype(BF16)
    o_ref[:, OFF_CQ:OFF_CQ + CA_W] = (proj(OFF_CQ, CA_W) * (CA_DH ** -0.5 * LOG2E)).astype(BF16)
    o_ref[:, OFF_CK:OFF_CK + CA_W] = proj(OFF_CK, CA_W).astype(BF16)
    o_ref[:, OFF_CV:OFF_CV + CA_W] = proj(OFF_CV, CA_W).astype(BF16)
    o_ref[:, OFF_DQ:OFF_DQ + DA_QK_W] = da_rotary(proj(OFF_DQ, DA_QK_W), DA_DH ** -0.5 * LOG2E).astype(BF16)
    o_ref[:, OFF_DK:OFF_DK + DA_QK_W] = da_rotary(proj(OFF_DK, DA_QK_W), 1.0).astype(BF16)
    o_ref[:, OFF_DV:OFF_DV + DA_V_W] = proj(OFF_DV, DA_V_W).astype(BF16)


def _inproj(layer, x2, w_in, tabs, seq, tm):
    t = x2.shape[0]
    pos_blocks = seq // tm
    tab_spec = pl.BlockSpec((tm, LANES), lambda i, l: (i % pos_blocks, 0))
    return pl.pallas_call(
        _inproj_kernel,
        grid_spec=pltpu.PrefetchScalarGridSpec(
            num_scalar_prefetch=1,
            grid=(t // tm,),
            in_specs=[pl.BlockSpec((tm, D_MODEL), lambda i, l: (i, 0)),
                      pl.BlockSpec((None, D_MODEL, MIX_W), lambda i, l: (l[0], 0, 0), pipeline_mode=pl.Buffered(1)),
                      tab_spec, tab_spec, tab_spec, tab_spec, tab_spec],
            out_specs=pl.BlockSpec((tm, MIX_W), lambda i, l: (i, 0))),
        out_shape=jax.ShapeDtypeStruct((t, MIX_W), BF16),
        compiler_params=_params(1),
        name="inproj",
    )(layer, x2, w_in, *tabs)


def _retention_kernel(q_ref, k_ref, v_ref, g_ref, dmat_ref, qdec_ref, kdec_ref, cdec_ref, ng_ref, o_ref,
                      state_ref, sv_ref, kv_ref):
    n_blocks = q_ref.shape[0] // RET_BLOCK

    def local(h, c):
        rows = slice(c * RET_BLOCK, (c + 1) * RET_BLOCK)
        q = q_ref[rows, h * RET_DK:(h + 1) * RET_DK]
        k = k_ref[rows, h * RET_DK:(h + 1) * RET_DK]
        v = v_ref[rows, h * RET_DV:(h + 1) * RET_DV]
        s = _dot_nt(q, k) * dmat_ref[h]
        sv_ref[c % 2, h] = _dot(s.astype(BF16), v)
        kd = (k.astype(F32) * kdec_ref[h]).astype(BF16)
        kv_ref[c % 2, h] = _dot_tn(kd, v)

    def finish(h, c):
        rows = slice(c * RET_BLOCK, (c + 1) * RET_BLOCK)
        cols = slice(h * RET_DV, (h + 1) * RET_DV)
        o = sv_ref[c % 2, h]
        if c > 0:
            qd = (q_ref[rows, h * RET_DK:(h + 1) * RET_DK].astype(F32) * qdec_ref[h]).astype(BF16)
            o = o + _dot(qd, state_ref[h].astype(BF16))
            state_ref[h] = state_ref[h] * cdec_ref[h, 0:1, :] + kv_ref[c % 2, h]
        else:
            state_ref[h] = kv_ref[c % 2, h]
        ms = jnp.mean(o * o, axis=-1, keepdims=True)
        y = o * lax.rsqrt(ms + LN_EPS) * ng_ref[...]
        half = 0.5 * g_ref[rows, cols].astype(F32)
        silu = half + half * jnp.tanh(half)
        o_ref[rows, cols] = (silu * y).astype(BF16)

    for h in range(RET_HEADS):
        local(h, 0)
    for c in range(n_blocks):
        for h in range(RET_HEADS):
            if c + 1 < n_blocks:
                local(h, c + 1)
            finish(h, c)


def _retention(h3, tabs, norm_g):
    b, s, _ = h3.shape
    dmat, qdec, kdec, cdec = tabs
    return pl.pallas_call(
        _retention_kernel,
        grid=(b,),
        in_specs=[pl.BlockSpec((None, s, RET_QK_W), lambda i: (i, 0, OFF_RQ // RET_QK_W)),
                  pl.BlockSpec((None, s, RET_QK_W), lambda i: (i, 0, OFF_RK // RET_QK_W)),
                  pl.BlockSpec((None, s, RET_V_W), lambda i: (i, 0, OFF_RV // RET_V_W)),
                  pl.BlockSpec((None, s, RET_V_W), lambda i: (i, 0, OFF_RG // RET_V_W)),
                  _const_spec((RET_HEADS, RET_BLOCK, RET_BLOCK)),
                  _const_spec((RET_HEADS, RET_BLOCK, RET_DK)),
                  _const_spec((RET_HEADS, RET_BLOCK, RET_DK)),
                  _const_spec((RET_HEADS, SUBLANES, RET_DV)),
                  _const_spec((1, RET_DV))],
        out_specs=pl.BlockSpec((None, s, RET_V_W), lambda i: (i, 0, 0)),
        out_shape=jax.ShapeDtypeStruct((b, s, RET_V_W), BF16),
        scratch_shapes=[pltpu.VMEM((RET_HEADS, RET_DK, RET_DV), F32),
                        pltpu.VMEM((2, RET_HEADS, RET_BLOCK, RET_DV), F32),
                        pltpu.VMEM((2, RET_HEADS, RET_DK, RET_DV), F32)],
        compiler_params=_params(1),
        name="retention",
    )(h3, h3, h3, h3, dmat, qdec, kdec, cdec, norm_g)


def _retention_tables():
    h = np.arange(RET_HEADS, dtype=np.float64)
    log_gamma = np.log1p(-np.exp2(-5.0 - h))
    n = np.arange(RET_BLOCK)
    diff = (n[:, None] - n[None, :]).astype(np.float64)
    cn, cm = n[:, None] // CHUNK, n[None, :] // CHUNK
    expo = np.where(cm == cn, np.abs(diff), diff)
    dmat = np.where(cm <= cn, np.exp(log_gamma[:, None, None] * expo[None]), 0.0)
    qdec = np.exp(log_gamma[:, None] * (n[None, :] + 1.0))
    kdec = np.exp(log_gamma[:, None] * (RET_BLOCK - 1.0 - n[None, :]))
    cdec = np.exp(log_gamma * RET_BLOCK)
    qdec = np.broadcast_to(qdec[:, :, None], (RET_HEADS, RET_BLOCK, RET_DK))
    kdec = np.broadcast_to(kdec[:, :, None], (RET_HEADS, RET_BLOCK, RET_DK))
    cdec = np.broadcast_to(cdec[:, None, None], (RET_HEADS, SUBLANES, RET_DV))
    return tuple(jnp.asarray(a, F32) for a in (dmat, qdec, kdec, cdec))


def _chunkattn_kernel(q_ref, k_ref, v_ref, bias_ref, o_ref, kpad_ref, vt_ref, s_ref, p_ref, ot_ref):
    s_len = q_ref.shape[0]
    n_blocks = s_len // CA_QBLOCK
    n_pad = CA_PAD // CA_QBLOCK
    kpad_ref[0:CA_PAD, :] = jnp.zeros((CA_PAD, CA_W), BF16)
    kpad_ref[CA_PAD:, :] = k_ref[...]
    ones = jnp.ones((ONES_ROWS, CA_QBLOCK), BF16)
    for c in range(n_pad + n_blocks):
        if c < n_pad:
            vt = jnp.zeros((CA_W, CA_QBLOCK), BF16)
        else:
            vt = v_ref[(c - n_pad) * CA_QBLOCK:(c - n_pad + 1) * CA_QBLOCK, :].T
        for h in range(CA_HEADS):
            vt_ref[c, h * CA_SLAB:h * CA_SLAB + CA_DH, :] = vt[h * CA_DH:(h + 1) * CA_DH, :]
            vt_ref[c, h * CA_SLAB + CA_DH:(h + 1) * CA_SLAB, :] = ones
    p_ref[:, CA_HWIN:CA_WIN, 0:CA_HALF] = jnp.zeros((2, CA_WIN - CA_HWIN, CA_HALF), BF16)
    p_ref[:, 0:CA_HALF, CA_HALF:CA_QBLOCK] = jnp.zeros((2, CA_HALF, CA_HALF), BF16)
    lane = lax.broadcasted_iota(jnp.int32, (CA_QBLOCK, LANES), 1)
    head_lanes = (lane < CA_DH, lane >= CA_DH)
    win_row = lax.broadcasted_iota(jnp.int32, (CA_HWIN, CA_HALF), 0)

    def scores(n):
        i, h = divmod(n, CA_HEADS)
        pair = slice((h // 2) * LANES, (h // 2 + 1) * LANES)
        q = q_ref[i * CA_QBLOCK:(i + 1) * CA_QBLOCK, pair]
        qz = jnp.where(head_lanes[h % 2], q, jnp.zeros_like(q))
        s_ref[n % CA_DEPTH] = _dot_nt(kpad_ref[i * CA_QBLOCK:i * CA_QBLOCK + CA_WIN, pair], qz)

    def attend(n):
        i, h = divmod(n, CA_HEADS)
        first_key = i * CA_QBLOCK - CA_PAD
        for half in range(2):
            rows = slice(half * CA_HALF, half * CA_HALF + CA_HWIN)
            cols = slice(half * CA_HALF, (half + 1) * CA_HALF)
            s = s_ref[n % CA_DEPTH, rows, cols] + bias_ref[h]
            if first_key + half * CA_HALF < 0:
                s = jnp.where(win_row >= -(first_key + half * CA_HALF), s, NEG_INF)
            m = jnp.max(s, axis=0, keepdims=True)
            p_ref[h % 2, rows, cols] = jnp.exp2(s - m).astype(BF16)
        pv = None
        for c in range(CA_WIN // CA_QBLOCK):
            part = _dot(vt_ref[i + c, h * CA_SLAB:(h + 1) * CA_SLAB, :],
                        p_ref[h % 2, c * CA_QBLOCK:(c + 1) * CA_QBLOCK, :])
            pv = part if pv is None else pv + part
        ot_ref[h * CA_DH:(h + 1) * CA_DH, :] = pv[0:CA_DH] * (1.0 / pv[CA_DH:CA_DH + 1])
        if h == CA_HEADS - 1:
            o_ref[i * CA_QBLOCK:(i + 1) * CA_QBLOCK, :] = ot_ref[...].T.astype(BF16)

    n_steps = n_blocks * CA_HEADS
    for n in range(n_steps + CA_AHEAD):
        if n < n_steps:
            scores(n)
        if n >= CA_AHEAD:
            attend(n - CA_AHEAD)


def _chunkattn(h3, bias):
    b, s, _ = h3.shape
    n_blocks = s // CA_QBLOCK
    return pl.pallas_call(
        _chunkattn_kernel,
        grid=(b,),
        in_specs=[pl.BlockSpec((None, s, CA_W), lambda i: (i, 0, OFF_CQ // CA_W)),
                  pl.BlockSpec((None, s, CA_W), lambda i: (i, 0, OFF_CK // CA_W)),
                  pl.BlockSpec((None, s, CA_W), lambda i: (i, 0, OFF_CV // CA_W)),
                  _const_spec((CA_HEADS, CA_HWIN, CA_HALF))],
        out_specs=pl.BlockSpec((None, s, CA_W), lambda i: (i, 0, 0)),
        out_shape=jax.ShapeDtypeStruct((b, s, CA_W), BF16),
        scratch_shapes=[pltpu.VMEM((s + CA_PAD, CA_W), BF16),
                        pltpu.VMEM((n_blocks + CA_PAD // CA_QBLOCK, CA_HEADS * CA_SLAB, CA_QBLOCK), BF16),
                        pltpu.VMEM((CA_DEPTH, CA_WIN, CA_QBLOCK), F32),
                        pltpu.VMEM((2, CA_WIN, CA_QBLOCK), BF16),
                        pltpu.VMEM((CA_W, CA_QBLOCK), F32)],
        compiler_params=_params(1),
        name="chunkattn",
    )(h3, h3, h3, bias)


def _chunkattn_bias(rel_bias):
    j = np.arange(CA_HWIN)[:, None]
    r = np.arange(CA_HALF)[None, :]
    qc, kc = r // CHUNK, j // CHUNK
    band = (kc >= qc) & (kc <= qc + CA_LEFT_CHUNKS)
    n_diag = CA_HWIN + CA_HALF
    u = np.arange(n_diag)
    d = np.where(u < CA_HALF, u, u - n_diag)
    idx = np.clip(d + CA_PAD, -REL_CLIP, REL_CLIP) + REL_CLIP
    diag = rel_bias[:, idx].astype(F32) * LOG2E
    shifted = jnp.tile(diag, (1, CA_HWIN))[:, :CA_HWIN * (n_diag - 1)].reshape(CA_HEADS, CA_HWIN, n_diag - 1)
    table = shifted[:, :, :CA_HALF]
    return jnp.where(jnp.asarray(band)[None], table, NEG_INF)


def _diffattn_kernel(q_ref, k_ref, v_ref, lq1_ref, lk1_ref, lq2_ref, lk2_ref, linit_ref, ng_ref, o_ref,
                     vt_ref, s_ref, pv_ref):
    s_len = q_ref.shape[0]
    n_blocks = s_len // DA_BLOCK
    hw = 2 * DA_DH
    lam = (jnp.exp(jnp.sum(lq1_ref[...] * lk1_ref[...], axis=-1, keepdims=True))
           - jnp.exp(jnp.sum(lq2_ref[...] * lk2_ref[...], axis=-1, keepdims=True)) + linit_ref[...])
    out_scale = ng_ref[...] * (1.0 - linit_ref[...])
    for c in range(n_blocks):
        blk = slice(c * DA_BLOCK, (c + 1) * DA_BLOCK)
        vt = v_ref[blk, :].T
        for pr in range(DA_PAIRS):
            vt_ref[pr, 0:hw, blk] = vt[pr * hw:(pr + 1) * hw, :]
    vt_ref[:, hw:, :] = jnp.ones((DA_PAIRS, ONES_ROWS, s_len), BF16)
    key_chunk = lax.broadcasted_iota(jnp.int32, (DA_BLOCK, DA_BLOCK), 0) // CHUNK
    query_chunk = lax.broadcasted_iota(jnp.int32, (DA_BLOCK, DA_BLOCK), 1) // CHUNK
    diag_allowed = key_chunk <= query_chunk
    lane = lax.broadcasted_iota(jnp.int32, (DA_BLOCK, hw), 1)
    head_lanes = (lane < DA_DH, lane >= DA_DH)

    steps = [(i, t, c, pr) for i in range(n_blocks) for t in range(2) for c in range(i + 1) for pr in range(DA_PAIRS)]
    maxima = {}
    masked_q = {}

    def scores(n):
        i, t, c, pr = steps[n]
        lanes = slice(pr * hw, (pr + 1) * hw)
        if c == 0:
            q = q_ref[i * DA_BLOCK:(i + 1) * DA_BLOCK, lanes]
            masked_q[pr] = jnp.where(head_lanes[t], q, jnp.zeros_like(q))
        s = _dot_nt(k_ref[c * DA_BLOCK:(c + 1) * DA_BLOCK, lanes], masked_q[pr])
        if c == i:
            s = jnp.where(diag_allowed, s, NEG_INF)
        s_ref[n % DA_RING] = s
        maxima[n] = jnp.max(s, axis=0, keepdims=True)

    def attend(n):
        i, t, c, pr = steps[n]
        p = jnp.exp2(s_ref[n % DA_RING] - maxima[n]).astype(BF16)
        pv_ref[pr, i % 2, t, c] = _dot(vt_ref[pr, :, c * DA_BLOCK:(c + 1) * DA_BLOCK], p)
        if c == i:
            mine = [n - (i - j) * DA_PAIRS for j in range(i + 1)]
            m = functools.reduce(jnp.maximum, [maxima[j] for j in mine])
            acc = None
            for j, nj in enumerate(mine):
                part = pv_ref[pr, i % 2, t, j] * jnp.exp2(maxima.pop(nj) - m)
                acc = part if acc is None else acc + part
            heads[pr][t] = acc[0:hw] * (1.0 / acc[hw:hw + 1])
            if t == 1:
                o_t = heads[pr][0] - lam * heads[pr][1]
                ms = jnp.mean(o_t * o_t, axis=0, keepdims=True)
                y = (o_t * lax.rsqrt(ms + LN_EPS)).T * out_scale
                o_ref[i * DA_BLOCK:(i + 1) * DA_BLOCK, pr * hw:(pr + 1) * hw] = y.astype(BF16)

    heads = [[None, None] for _ in range(DA_PAIRS)]
    for n in range(len(steps) + DA_AHEAD):
        if n < len(steps):
            scores(n)
        if n >= DA_AHEAD:
            attend(n - DA_AHEAD)


def _diffattn(h3, lams, norm_g, lambda_init):
    b, s, _ = h3.shape
    hw = 2 * DA_DH
    bw = DA_PAIRS * hw
    small = pl.BlockSpec((1, DA_DH), lambda i, h: (0, 0))
    return pl.pallas_call(
        _diffattn_kernel,
        grid=(b, DA_HEADS // DA_PAIRS),
        in_specs=[pl.BlockSpec((None, s, bw), lambda i, h: (i, 0, OFF_DQ // bw + h)),
                  pl.BlockSpec((None, s, bw), lambda i, h: (i, 0, OFF_DK // bw + h)),
                  pl.BlockSpec((None, s, bw), lambda i, h: (i, 0, OFF_DV // bw + h)),
                  small, small, small, small,
                  pl.BlockSpec((1, 1), lambda i, h: (0, 0)),
                  pl.BlockSpec((1, hw), lambda i, h: (0, 0))],
        out_specs=pl.BlockSpec((None, s, bw), lambda i, h: (i, 0, h)),
        out_shape=jax.ShapeDtypeStruct((b, s, DA_V_W), BF16),
        scratch_shapes=[pltpu.VMEM((DA_PAIRS, hw + ONES_ROWS, s), BF16),
                        pltpu.VMEM((DA_RING, DA_BLOCK, DA_BLOCK), F32),
                        pltpu.VMEM((DA_PAIRS, 2, 2, s // DA_BLOCK, hw + ONES_ROWS, DA_BLOCK), F32)],
        compiler_params=_params(2),
        name="diffattn",
    )(h3, h3, h3, *lams, jnp.full((1, 1), lambda_init, F32), norm_g)


def _layer_norm(y, g, b):
    mu = jnp.mean(y, axis=-1, keepdims=True)
    d = y - mu
    var = jnp.mean(d * d, axis=-1, keepdims=True)
    return d * lax.rsqrt(var + LN_EPS) * g + b


def _merge_kernel(layer_ref, x_ref, oa_ref, ob_ref, oc_ref, wg_ref, bg_ref, wa_ref, wb_ref, wc_ref, wo_ref, g_ref, b_ref,
                  o_ref):
    for r0 in range(0, x_ref.shape[0], SUB_ROWS):
        rows = slice(r0, r0 + SUB_ROWS)
        xf = x_ref[rows, :]
        xb = xf.astype(BF16)
        merged = None
        for n, (br_ref, w_ref) in enumerate(((oa_ref, wa_ref), (ob_ref, wb_ref), (oc_ref, wc_ref))):
            cols = slice(n * D_MODEL, (n + 1) * D_MODEL)
            z = _dot(xb, wg_ref[:, cols]) + bg_ref[:, cols]
            term = _dot(br_ref[rows, :], w_ref[...]) * (1.0 / (1.0 + jnp.exp(-z)))
            merged = term if merged is None else merged + term
        mix = _dot(merged.astype(BF16), wo_ref[...])
        o_ref[rows, :] = _layer_norm(DEEPNORM_ALPHA * xf + mix, g_ref[...], b_ref[...])


def _merge(layer, x2, oa, ob, oc, w_in, bg, wa, wb, wc, wo, g, b, tm):
    t = x2.shape[0]
    row = lambda w: pl.BlockSpec((tm, w), lambda i, l: (i, 0))
    return pl.pallas_call(
        _merge_kernel,
        grid_spec=pltpu.PrefetchScalarGridSpec(
            num_scalar_prefetch=1,
            grid=(t // tm,),
            in_specs=[row(D_MODEL), row(RET_V_W), row(CA_W), row(DA_V_W),
                      pl.BlockSpec((None, D_MODEL, GATE_W), lambda i, l: (l[0], 0, MIX_W // GATE_W),
                                   pipeline_mode=pl.Buffered(1)),
                      _const_spec((1, GATE_W)),
                      _const_spec((RET_V_W, D_MODEL)), _const_spec((CA_W, D_MODEL)), _const_spec((DA_V_W, D_MODEL)),
                      _const_spec((D_MODEL, D_MODEL)), _const_spec((1, D_MODEL)), _const_spec((1, D_MODEL))],
            out_specs=row(D_MODEL)),
        out_shape=jax.ShapeDtypeStruct((t, D_MODEL), F32),
        compiler_params=_params(1),
        name="merge",
    )(layer, x2, oa, ob, oc, w_in, bg, wa, wb, wc, wo, g, b)


def _ffn_kernel(x_ref, wi_ref, wo_ref, g_ref, b_ref, o_ref, act_ref):
    for r0 in range(0, x_ref.shape[0], SUB_ROWS):
        rows = slice(r0, r0 + SUB_ROWS)
        xf = x_ref[rows, :]
        xb = xf.astype(BF16)
        for c in range(0, FFN_HIDDEN, FFN_CHUNK):
            ug = _dot(xb, wi_ref[:, c:c + FFN_CHUNK])
            uu = _dot(xb, wi_ref[:, FFN_HIDDEN + c:FFN_HIDDEN + c + FFN_CHUNK])
            half = 0.5 * ug
            silu = half + half * jnp.tanh(half)
            act_ref[rows, c:c + FFN_CHUNK] = (silu * uu).astype(BF16)
        ffn = _dot(act_ref[rows, :], wo_ref[...])
        o_ref[rows, :] = _layer_norm(DEEPNORM_ALPHA * xf + ffn, g_ref[...], b_ref[...])


def _ffn(x2, wi, wo, g, b, tm):
    t = x2.shape[0]
    return pl.pallas_call(
        _ffn_kernel,
        grid=(t // tm,),
        in_specs=[pl.BlockSpec((tm, D_MODEL), lambda i: (i, 0)),
                  _const_spec((D_MODEL, 2 * FFN_HIDDEN)), _const_spec((FFN_HIDDEN, D_MODEL)),
                  _const_spec((1, D_MODEL)), _const_spec((1, D_MODEL))],
        out_specs=pl.BlockSpec((tm, D_MODEL), lambda i: (i, 0)),
        out_shape=jax.ShapeDtypeStruct((t, D_MODEL), F32),
        scratch_shapes=[pltpu.VMEM((tm, FFN_HIDDEN), BF16)],
        compiler_params=_params(1),
        name="ffn",
    )(x2, wi, wo, g, b)


def _rotary_tables(seq):
    pos = jnp.arange(seq, dtype=F32)[:, None]

    def cs(d):
        inv_freq = ROPE_THETA ** (-jnp.arange(0, d, 2, dtype=F32) / d)
        ang = pos * inv_freq[None, :]
        return jnp.cos(ang), jnp.sin(ang)

    rc, rs = cs(RET_DK)
    rcos = jnp.concatenate([rc, rc], axis=1)
    rsin = jnp.concatenate([-rs, rs], axis=1)
    dc, ds = cs(DA_DH)
    z = jnp.zeros_like(ds)
    dcos = jnp.concatenate([dc, dc, dc, dc], axis=1)
    dsina = jnp.concatenate([-ds, z, -ds, z], axis=1)
    dsinb = jnp.concatenate([z, ds, z, ds], axis=1)
    return rcos, rsin, dcos, dsina, dsinb


def kernel(x, w_in, ret_norm_g, ca_rel_bias, da_lambda_q1, da_lambda_k1, da_lambda_q2, da_lambda_k2, da_norm_g, w_branch_a, w_branch_b, w_branch_c, b_merge, w_out, ln1_g, ln1_b, w_ffn_in, w_ffn_out, ln2_g, ln2_b):
    b, s, d = x.shape
    t = b * s
    assert d == D_MODEL and w_in.shape == (DEPTH, D_MODEL, MIX_W + GATE_W)
    assert s % PROJ_ROWS == 0 and t % TAIL_ROWS == 0
    assert s % RET_BLOCK == 0 and s % DA_BLOCK == 0 and s % CA_QBLOCK == 0 and s >= CA_WIN
    assert MIX_W % GATE_W == 0 and DA_HEADS % DA_PAIRS == 0
    rot_tabs = _rotary_tables(s)
    ret_tabs = _retention_tables()
    row = lambda a: a.reshape(1, -1).astype(F32)

    w_in_b = w_in.astype(BF16)
    x2 = x.reshape(t, d)
    for l in range(DEPTH):
        lambda_init = 0.8 - 0.6 * math.exp(-0.3 * l)
        layer = jnp.full((1,), l, jnp.int32)
        h = _inproj(layer, x2, w_in_b, rot_tabs, s, PROJ_ROWS)
        h3 = h.reshape(b, s, MIX_W)
        o_a = _retention(h3, ret_tabs, row(ret_norm_g[l]))
        o_b = _chunkattn(h3, _chunkattn_bias(ca_rel_bias[l]))
        o_c = _diffattn(h3, (row(da_lambda_q1[l]), row(da_lambda_k1[l]), row(da_lambda_q2[l]), row(da_lambda_k2[l])),
                        row(da_norm_g[l]), lambda_init)
        x2 = _merge(layer, x2, o_a.reshape(t, RET_V_W), o_b.reshape(t, CA_W), o_c.reshape(t, DA_V_W),
                    w_in_b, row(b_merge[l]), w_branch_a[l].astype(BF16), w_branch_b[l].astype(BF16),
                    w_branch_c[l].astype(BF16), w_out[l].astype(BF16), row(ln1_g[l]), row(ln1_b[l]), TAIL_ROWS)
        x2 = _ffn(x2, w_ffn_in[l].astype(BF16), w_ffn_out[l].astype(BF16), row(ln2_g[l]), row(ln2_b[l]), TAIL_ROWS)
    return x2.reshape(b, s, d)
```
